```python
import math
import jax, jax.numpy as jnp
from jax import lax
import numpy as np

D_MODEL = 2048
BATCH = 16
SEQ = 2048
DEPTH = 2
DEC_BATCH = 2
DEC_SEQ = 16384
PAST_LEN = 128

N_MIXERS = 2
N_A_LAYERS = (DEPTH + 1) // 2
N_B_LAYERS = DEPTH // 2
CHUNK = 128
GMLP_WIDTH = D_MODEL
GMLP_GROUPS = 16
GMLP_GROUP_DIM = GMLP_WIDTH // GMLP_GROUPS
HEAD_DIM = 128
HEADS_PER_GROUP = 8
DILATION_PAIRS = ((128, 1), (512, 4), (2048, 16))
N_DIL_GROUPS = len(DILATION_PAIRS)
ATTN_WIDTH = HEADS_PER_GROUP * HEAD_DIM
ROT_DIM = HEAD_DIM // 4
ROPE_THETA = 500000.0
NEG_BIG = -1e30
D_FF = ((8 * D_MODEL + 3 * 256 - 1) // (3 * 256)) * 256
PLE_DIM = 256
DEEPNORM_ALPHA = (2.0 * DEPTH) ** 0.25
DEEPNORM_BETA = (8.0 * DEPTH) ** -0.25
LN_EPS = 1e-5

kernel_name = "hybrid_gmlp_dilated_attn_encoder"


def layer_norm(x, g, b):
    xf = x.astype(jnp.float32)
    mu = jnp.mean(xf, axis=-1, keepdims=True)
    var = jnp.mean(jnp.square(xf - mu), axis=-1, keepdims=True)
    y = (xf - mu) * lax.rsqrt(var + LN_EPS)
    return (y * g.astype(jnp.float32) + b.astype(jnp.float32)).astype(x.dtype)


def rotary_partial(t, pos):
    half = ROT_DIM // 2
    inv_freq = ROPE_THETA ** (-jnp.arange(0, ROT_DIM, 2, dtype=jnp.float32) / ROT_DIM)
    ang = pos.astype(jnp.float32)[:, None] * inv_freq[None, :]
    cos = jnp.cos(ang)[None, :, None, :]
    sin = jnp.sin(ang)[None, :, None, :]
    tr = t[..., :ROT_DIM].astype(jnp.float32)
    t1, t2 = tr[..., :half], tr[..., half:]
    rot = jnp.concatenate([t1 * cos - t2 * sin, t1 * sin + t2 * cos], axis=-1)
    return jnp.concatenate([rot.astype(t.dtype), t[..., ROT_DIM:]], axis=-1)


def banded_attention(q, k, v, half):
    bp, L, H, dh = q.shape
    blk = 2 * half
    nb = -(-L // blk)
    lp = nb * blk
    qp = jnp.pad(q, ((0, 0), (0, lp - L), (0, 0), (0, 0))).reshape(bp, nb, blk, H, dh)
    pad_k = ((0, 0), (half, lp - L + half), (0, 0), (0, 0))
    kp = jnp.pad(k, pad_k).reshape(bp, nb + 1, blk, H, dh)
    vp = jnp.pad(v, pad_k).reshape(bp, nb + 1, blk, H, dh)
    kwin = jnp.concatenate([kp[:, :-1], kp[:, 1:]], axis=2)
    vwin = jnp.concatenate([vp[:, :-1], vp[:, 1:]], axis=2)
    s = jnp.einsum('bnqhd,bnkhd->bnhqk', qp.astype(jnp.float32), kwin.astype(jnp.float32))
    a = jnp.arange(blk)[:, None]
    c = jnp.arange(2 * blk)[None, :]
    rel = c - a
    band = (rel >= 0) & (rel <= 2 * half)
    j = jnp.arange(nb)[:, None, None] * blk + c[None] - half
    mask = band[None] & (j >= 0) & (j < L)
    s = jnp.where(mask[None, :, None], s, NEG_BIG)
    m = jnp.max(s, axis=-1, keepdims=True)
    p = jnp.exp(s - m)
    den = jnp.sum(p, axis=-1)
    o = jnp.einsum('bnhqk,bnkhd->bnqhd', p, vwin.astype(jnp.float32))
    o = o / jnp.transpose(den, (0, 1, 3, 2))[..., None]
    lse = jnp.transpose(m[..., 0] + jnp.log(den), (0, 1, 3, 2))
    o = o.reshape(bp, lp, H, dh)[:, :L]
    lse = lse.reshape(bp, lp, H)[:, :L]
    return o, lse


def dilated_group(q, k, v, dil, half):
    b, s, h, dh = q.shape
    L = s // dil

    def to_sub(t):
        return jnp.transpose(t.reshape(b, L, dil, h, dh), (0, 2, 1, 3, 4)).reshape(b * dil, L, h, dh)

    o, lse = banded_attention(to_sub(q), to_sub(k), to_sub(v), half)
    o = jnp.transpose(o.reshape(b, dil, L, h, dh), (0, 2, 1, 3, 4)).reshape(b, s, h, dh)
    lse = jnp.transpose(lse.reshape(b, dil, L, h), (0, 2, 1, 3)).reshape(b, s, h)
    return o, lse


def dilated_attention(x, w_qkv, w_o, pos):
    b, s, _ = x.shape
    qkv = (x @ w_qkv).reshape(b, s, N_DIL_GROUPS, 3, HEADS_PER_GROUP, HEAD_DIM)
    outs, lses = [], []
    for g, (window, dil) in enumerate(DILATION_PAIRS):
        q = rotary_partial(qkv[:, :, g, 0], pos) * (HEAD_DIM ** -0.5)
        k = rotary_partial(qkv[:, :, g, 1], pos)
        v = qkv[:, :, g, 2]
        o, lse = dilated_group(q, k, v, dil, window // (2 * dil))
        outs.append(o)
        lses.append(lse)
    wts = jax.nn.softmax(jnp.stack(lses, axis=0), axis=0)
    o = sum(wts[g][..., None] * outs[g] for g in range(N_DIL_GROUPS))
    return o.astype(x.dtype).reshape(b, s, ATTN_WIDTH) @ w_o


def chunked_gmlp(x, w_in, ln_g, ln_b, w_s, b_s, w_o):
    b, s, _ = x.shape
    h = jax.nn.gelu(x @ w_in, approximate=False)
    u, v = jnp.split(h, 2, axis=-1)
    v = layer_norm(v, ln_g, ln_b).reshape(b, s // CHUNK, CHUNK, GMLP_GROUPS, GMLP_GROUP_DIM)
    v = jnp.einsum('gpq,bnqgc->bnpgc', w_s, v) + jnp.transpose(b_s)[:, :, None]
    return (u * v.reshape(b, s, GMLP_WIDTH)) @ w_o


def swiglu(x, w_gu, w_down):
    g, u = jnp.split(x @ w_gu, 2, axis=-1)
    return (jax.nn.silu(g) * u) @ w_down


def trunk(x, p, w_in_a, ln_v_g, ln_v_b, w_s_a, b_s_a, w_o_a, w_qkv_b, w_o_b,
          ln_mix_g, ln_mix_b, w_ffn_gu, w_ffn_down, ln_ffn_g, ln_ffn_b,
          w_ple_gate, w_ple_proj):
    pos = jnp.arange(x.shape[1], dtype=jnp.float32)
    for i in range(DEPTH):
        j = i // N_MIXERS
        if i % N_MIXERS == 0:
            mix = chunked_gmlp(x, w_in_a[j], ln_v_g[j], ln_v_b[j], w_s_a[j], b_s_a[j], w_o_a[j])
        else:
            mix = dilated_attention(x, w_qkv_b[j], w_o_b[j], pos)
        x = layer_norm(DEEPNORM_ALPHA * x + mix, ln_mix_g[i], ln_mix_b[i])
        x = layer_norm(DEEPNORM_ALPHA * x + swiglu(x, w_ffn_gu[i], w_ffn_down[i]), ln_ffn_g[i], ln_ffn_b[i])
        x = x + jax.nn.sigmoid(x @ w_ple_gate[i]) * (p[i] @ w_ple_proj[i])
    return x


def setup_inputs(seed: int = 0) -> dict:
    key = jax.random.key(seed)
    ks = jax.random.split(key, 24)
    f32 = jnp.float32

    def nrm(k, shape, scale):
        return jax.random.normal(k, shape, f32) * scale

    return {
        "x_prompt": nrm(ks[0], (BATCH, SEQ, D_MODEL), 1.0),
        "x_sample": nrm(ks[1], (DEC_BATCH, DEC_SEQ, D_MODEL), 1.0),
        "p_prompt": nrm(ks[2], (DEPTH, BATCH, SEQ, PLE_DIM), 1.0),
        "p_sample": nrm(ks[3], (DEPTH, DEC_BATCH, DEC_SEQ, PLE_DIM), 1.0),
        "w_in_a": nrm(ks[4], (N_A_LAYERS, D_MODEL, 2 * GMLP_WIDTH), D_MODEL ** -0.5),
        "ln_v_g": 1.0 + nrm(ks[5], (N_A_LAYERS, GMLP_WIDTH), 0.02),
        "ln_v_b": nrm(ks[6], (N_A_LAYERS, GMLP_WIDTH), 0.02),
        "w_s_a": nrm(ks[7], (N_A_LAYERS, GMLP_GROUPS, CHUNK, CHUNK), CHUNK ** -0.5),
        "b_s_a": 1.0 + nrm(ks[8], (N_A_LAYERS, GMLP_GROUPS, CHUNK), 0.01),
        "w_o_a": nrm(ks[9], (N_A_LAYERS, GMLP_WIDTH, D_MODEL), DEEPNORM_BETA * GMLP_WIDTH ** -0.5),
        "w_qkv_b": nrm(ks[10], (N_B_LAYERS, D_MODEL, N_DIL_GROUPS * 3 * ATTN_WIDTH), D_MODEL ** -0.5),
        "w_o_b": nrm(ks[11], (N_B_LAYERS, ATTN_WIDTH, D_MODEL), DEEPNORM_BETA * ATTN_WIDTH ** -0.5),
        "ln_mix_g": 1.0 + nrm(ks[12], (DEPTH, D_MODEL), 0.02),
        "ln_mix_b": nrm(ks[13], (DEPTH, D_MODEL), 0.02),
        "w_ffn_gu": nrm(ks[14], (DEPTH, D_MODEL, 2 * D_FF), D_MODEL ** -0.5),
        "w_ffn_down": nrm(ks[15], (DEPTH, D_FF, D_MODEL), DEEPNORM_BETA * D_FF ** -0.5),
        "ln_ffn_g": 1.0 + nrm(ks[16], (DEPTH, D_MODEL), 0.02),
        "ln_ffn_b": nrm(ks[17], (DEPTH, D_MODEL), 0.02),
        "w_ple_gate": nrm(ks[18], (DEPTH, D_MODEL, D_MODEL), D_MODEL ** -0.5),
        "w_ple_proj": nrm(ks[19], (DEPTH, PLE_DIM, D_MODEL), 0.5 * PLE_DIM ** -0.5),
    }


def reference(x_prompt, x_sample, p_prompt, p_sample, w_in_a, ln_v_g, ln_v_b, w_s_a, b_s_a,
              w_o_a, w_qkv_b, w_o_b, ln_mix_g, ln_mix_b, w_ffn_gu, w_ffn_down, ln_ffn_g,
              ln_ffn_b, w_ple_gate, w_ple_proj):
    y_prompt = trunk(x_prompt, p_prompt, w_in_a, ln_v_g, ln_v_b, w_s_a, b_s_a, w_o_a, w_qkv_b,
                     w_o_b, ln_mix_g, ln_mix_b, w_ffn_gu, w_ffn_down, ln_ffn_g, ln_ffn_b,
                     w_ple_gate, w_ple_proj)
    y_sample = trunk(x_sample, p_sample, w_in_a, ln_v_g, ln_v_b, w_s_a, b_s_a, w_o_a, w_qkv_b,
                     w_o_b, ln_mix_g, ln_mix_b, w_ffn_gu, w_ffn_down, ln_ffn_g, ln_ffn_b,
                     w_ple_gate, w_ple_proj)
    return (y_prompt, y_sample)
```

```python
import functools
import math

import jax
import jax.numpy as jnp
from jax import lax
from jax.experimental import pallas as pl
from jax.experimental.pallas import tpu as pltpu

F32 = jnp.float32
BF16 = jnp.bfloat16

LN_EPS = 1e-5
ROPE_THETA = 500000.0
NEG_BIG = -1e30
HEAD_DIM = 128
ROT_DIM = HEAD_DIM // 4
CHUNK = 128
GROUP_DIM = 128
DILATION_PAIRS = ((128, 1), (512, 4), (2048, 16))
N_QKV = 3

V7X_VMEM_LIMIT_BYTES = 56 * 1024 * 1024
LANES = 128


def _params(*semantics):
    return pltpu.CompilerParams(dimension_semantics=semantics,
                                vmem_limit_bytes=V7X_VMEM_LIMIT_BYTES)


def _resident(shape):
    nd = len(shape)
    return pl.BlockSpec(shape, lambda *_: (0,) * nd, pipeline_mode=pl.Buffered(1))


def _row_tile(m, pref):
    t = min(m, pref)
    assert m % t == 0, (m, t)
    return t


def _col_chunk(n, pref=512):
    c = min(n, pref)
    assert n % c == 0, (n, c)
    return c


def _layer_norm(y, g, b):
    mu = jnp.mean(y, axis=-1, keepdims=True)
    yc = y - mu
    var = jnp.mean(yc * yc, axis=-1, keepdims=True)
    return yc * lax.rsqrt(var + LN_EPS) * g + b


def _gelu(h):
    return 0.5 * h * (1.0 + lax.erf(h * math.sqrt(0.5)))


def _sigmoid(z):
    return 1.0 / (1.0 + jnp.exp(-z))


def _mm(a, b):
    return jnp.dot(a, b, preferred_element_type=F32)


def _gmlp_in_kernel(x_ref, w_ref, g_ref, b_ref, u_ref, v_ref, hv_ref, *, width, nc):
    xb = x_ref[...].astype(BF16)
    for n in range(0, width, nc):
        u_ref[:, n:n + nc] = _gelu(_mm(xb, w_ref[:, n:n + nc]))
    for n in range(0, width, nc):
        hv_ref[:, n:n + nc] = _gelu(_mm(xb, w_ref[:, width + n:width + n + nc]))
    v_ref[...] = _layer_norm(hv_ref[...], g_ref[...], b_ref[...]).astype(BF16)


def _gmlp_in(x, w_in, ln_g, ln_b):
    m, d = x.shape
    width = w_in.shape[1] // 2
    tm = _row_tile(m, 256)
    kern = functools.partial(_gmlp_in_kernel, width=width, nc=_col_chunk(width))
    return pl.pallas_call(
        kern,
        grid=(m // tm,),
        in_specs=[
            pl.BlockSpec((tm, d), lambda i: (i, 0)),
            _resident(w_in.shape),
            _resident(ln_g.shape),
            _resident(ln_b.shape),
        ],
        out_specs=[
            pl.BlockSpec((tm, width), lambda i: (i, 0)),
            pl.BlockSpec((tm, width), lambda i: (i, 0)),
        ],
        out_shape=[
            jax.ShapeDtypeStruct((m, width), F32),
            jax.ShapeDtypeStruct((m, width), BF16),
        ],
        scratch_shapes=[pltpu.VMEM((tm, width), F32)],
        compiler_params=_params("parallel"),
        name="gmlp_in",
    )(x, w_in, ln_g, ln_b)


def _gmlp_out_kernel(u_ref, v_ref, x_ref, ws_ref, bias_ref, wo_ref, g_ref, b_ref,
                     o_ref, t_ref, *, groups, alpha, nc):
    tm, d = o_ref.shape
    for c in range(tm // CHUNK):
        rows = slice(c * CHUNK, (c + 1) * CHUNK)
        for g in range(groups):
            cols = slice(g * GROUP_DIM, (g + 1) * GROUP_DIM)
            sv = _mm(ws_ref[g], v_ref[rows, cols]) + bias_ref[:, cols]
            t_ref[rows, cols] = (u_ref[rows, cols] * sv).astype(BF16)
    for n in range(0, d, nc):
        o_ref[:, n:n + nc] = alpha * x_ref[:, n:n + nc] + _mm(t_ref[...], wo_ref[:, n:n + nc])
    o_ref[...] = _layer_norm(o_ref[...], g_ref[...], b_ref[...])


def _gmlp_out(u, v, x, w_s, bias, w_o, ln_g, ln_b, alpha):
    m, d = x.shape
    width = u.shape[1]
    groups = w_s.shape[0]
    tm = _row_tile(m, 256)
    kern = functools.partial(_gmlp_out_kernel, groups=groups, alpha=alpha, nc=_col_chunk(d))
    return pl.pallas_call(
        kern,
        grid=(m // tm,),
        in_specs=[
            pl.BlockSpec((tm, width), lambda i: (i, 0)),
            pl.BlockSpec((tm, width), lambda i: (i, 0)),
            pl.BlockSpec((tm, d), lambda i: (i, 0)),
            _resident(w_s.shape),
            _resident(bias.shape),
            _resident(w_o.shape),
            _resident(ln_g.shape),
            _resident(ln_b.shape),
        ],
        out_specs=pl.BlockSpec((tm, d), lambda i: (i, 0)),
        out_shape=jax.ShapeDtypeStruct((m, d), F32),
        scratch_shapes=[pltpu.VMEM((tm, width), BF16)],
        compiler_params=_params("parallel"),
        name="gmlp_out",
    )(u, v, x, w_s, bias, w_o, ln_g, ln_b)


def _ffn_kernel(x_ref, wg_ref, wu_ref, wd_ref, g_ref, b_ref, o_ref, xb_ref, acc_ref, *, alpha):
    f = pl.program_id(1)

    @pl.when(f == 0)
    def _():
        xb_ref[...] = x_ref[...].astype(BF16)

    xb = xb_ref[...]
    gate = _mm(xb, wg_ref[...])
    up = _mm(xb, wu_ref[...])
    act = (gate * _sigmoid(gate) * up).astype(BF16)
    part = _mm(act, wd_ref[...])

    @pl.when(f == 0)
    def _():
        acc_ref[...] = part

    @pl.when(f > 0)
    def _():
        acc_ref[...] += part

    @pl.when(f == pl.num_programs(1) - 1)
    def _():
        y = alpha * x_ref[...] + acc_ref[...]
        o_ref[...] = _layer_norm(y, g_ref[...], b_ref[...])


def _ffn(x, w_gu, w_down, ln_g, ln_b, alpha):
    m, d = x.shape
    dff = w_down.shape[0]
    tm = _row_tile(m, 512)
    tf = _col_chunk(dff, 512)
    nf = dff // tf
    kern = functools.partial(_ffn_kernel, alpha=alpha)
    return pl.pallas_call(
        kern,
        grid=(m // tm, nf),
        in_specs=[
            pl.BlockSpec((tm, d), lambda i, f: (i, 0)),
            pl.BlockSpec((d, tf), lambda i, f: (0, f)),
            pl.BlockSpec((d, tf), lambda i, f: (0, nf + f)),
            pl.BlockSpec((tf, d), lambda i, f: (f, 0)),
            _resident(ln_g.shape),
            _resident(ln_b.shape),
        ],
        out_specs=pl.BlockSpec((tm, d), lambda i, f: (i, 0)),
        out_shape=jax.ShapeDtypeStruct((m, d), F32),
        scratch_shapes=[pltpu.VMEM((tm, d), BF16), pltpu.VMEM((tm, d), F32)],
        compiler_params=_params("parallel", "arbitrary"),
        name="ffn",
    )(x, w_gu, w_gu, w_down, ln_g, ln_b)


def _ple_kernel(x_ref, p_ref, wg_ref, wp_ref, o_ref, *, nc):
    d = o_ref.shape[1]
    xb = x_ref[...].astype(BF16)
    pb = p_ref[...].astype(BF16)
    for n in range(0, d, nc):
        gate = _sigmoid(_mm(xb, wg_ref[:, n:n + nc]))
        o_ref[:, n:n + nc] = x_ref[:, n:n + nc] + gate * _mm(pb, wp_ref[:, n:n + nc])


def _ple(x, p, w_gate, w_proj):
    m, d = x.shape
    pd = p.shape[1]
    tm = _row_tile(m, 512)
    kern = functools.partial(_ple_kernel, nc=_col_chunk(d))
    return pl.pallas_call(
        kern,
        grid=(m // tm,),
        in_specs=[
            pl.BlockSpec((tm, d), lambda i: (i, 0)),
            pl.BlockSpec((tm, pd), lambda i: (i, 0)),
            _resident(w_gate.shape),
            _resident(w_proj.shape),
        ],
        out_specs=pl.BlockSpec((tm, d), lambda i: (i, 0)),
        out_shape=jax.ShapeDtypeStruct((m, d), F32),
        compiler_params=_params("parallel"),
        name="ple",
    )(x, p, w_gate, w_proj)


def _rope_tables(seq):
    half = ROT_DIM // 2
    inv_freq = ROPE_THETA ** (-jnp.arange(0, ROT_DIM, 2, dtype=F32) / ROT_DIM)
    ang = jnp.arange(seq, dtype=F32)[:, None] * inv_freq[None, :]
    cos, sin = jnp.cos(ang), jnp.sin(ang)
    rest = HEAD_DIM - ROT_DIM
    c = jnp.concatenate([cos, cos, jnp.ones((seq, rest), F32)], axis=1)
    sa = jnp.concatenate([-sin, jnp.zeros((seq, HEAD_DIM - half), F32)], axis=1)
    sb = jnp.concatenate([jnp.zeros((seq, half), F32), sin, jnp.zeros((seq, rest), F32)], axis=1)
    return c, sa, sb


def _qkv_kernel(x_ref, w_ref, c_ref, sa_ref, sb_ref, o_ref, xb_ref, *, heads):
    j = pl.program_id(1)
    kind = j % N_QKV

    @pl.when(j == 0)
    def _():
        xb_ref[...] = x_ref[...].astype(BF16)

    h = _mm(xb_ref[...], w_ref[...])

    @pl.when(kind == 2)
    def _():
        o_ref[...] = h.astype(BF16)

    @pl.when(kind != 2)
    def _():
        half = ROT_DIM // 2
        scale = jnp.where(kind == 0, HEAD_DIM ** -0.5, 1.0).astype(F32)
        c, sa, sb = c_ref[...], sa_ref[...], sb_ref[...]
        for hd in range(heads):
            cols = slice(hd * HEAD_DIM, (hd + 1) * HEAD_DIM)
            t = h[:, cols]
            rot = (t * c + pltpu.roll(t, HEAD_DIM - half, 1) * sa + pltpu.roll(t, half, 1) * sb)
            o_ref[:, cols] = (rot * scale).astype(BF16)


def _qkv(x, w_qkv, tables, seq, heads):
    m, d = x.shape
    n = w_qkv.shape[1]
    aw = heads * HEAD_DIM
    tm = _row_tile(seq, 1024)
    spt = seq // tm
    kern = functools.partial(_qkv_kernel, heads=heads)
    tab_spec = pl.BlockSpec((tm, HEAD_DIM), lambda i, j: (i % spt, 0))
    return pl.pallas_call(
        kern,
        grid=(m // tm, n // aw),
        in_specs=[
            pl.BlockSpec((tm, d), lambda i, j: (i, 0)),
            pl.BlockSpec((d, aw), lambda i, j: (0, j)),
            tab_spec, tab_spec, tab_spec,
        ],
        out_specs=pl.BlockSpec((tm, aw), lambda i, j: (i, j)),
        out_shape=jax.ShapeDtypeStruct((m, n), BF16),
        scratch_shapes=[pltpu.VMEM((tm, d), BF16)],
        compiler_params=_params("parallel", "arbitrary"),
        name="qkv_rope",
    )(x, w_qkv, *tables)


def _band_attn_kernel(q_ref, kp_ref, kc_ref, kn_ref, vp_ref, vc_ref, vn_ref,
                      o_ref, lse_ref, kcat_ref, vcat_ref, *, heads, sub_len, half):
    tq = q_ref.shape[0]
    nk = tq + 2 * half
    blk = pl.program_id(2)
    kcat_ref[0:half] = kp_ref[...]
    kcat_ref[half:half + tq] = kc_ref[...]
    kcat_ref[half + tq:nk] = kn_ref[...]
    vcat_ref[0:half] = vp_ref[...]
    vcat_ref[half:half + tq] = vc_ref[...]
    vcat_ref[half + tq:nk] = vn_ref[...]

    row = lax.broadcasted_iota(jnp.int32, (tq, nk), 0)
    col = lax.broadcasted_iota(jnp.int32, (tq, nk), 1)
    rel = col - row
    key = blk * tq - half + col
    mask = (rel >= 0) & (rel <= 2 * half) & (key >= 0) & (key < sub_len)

    lane = lax.broadcasted_iota(jnp.int32, (tq, LANES), 1)
    lse_all = jnp.zeros((tq, LANES), F32)
    for hd in range(heads):
        cols = slice(hd * HEAD_DIM, (hd + 1) * HEAD_DIM)
        s = lax.dot_general(q_ref[:, cols], kcat_ref[:, cols],
                            (((1,), (1,)), ((), ())), preferred_element_type=F32)
        s = jnp.where(mask, s, NEG_BIG)
        mx = jnp.max(s, axis=-1, keepdims=True)
        p = jnp.exp(s - mx)
        den = jnp.sum(p, axis=-1, keepdims=True)
        o = _mm(p.astype(BF16), vcat_ref[:, cols])
        o_ref[:, cols] = o / den
        lse_all = jnp.where(lane == hd, mx + jnp.log(den), lse_all)
    lse_ref[...] = lse_all


def _band_attn(qkv, group, dil, half, batch, seq, heads):
    m, n = qkv.shape
    aw = heads * HEAD_DIM
    nblk = n // aw
    sub_len = seq // dil
    tq = _row_tile(sub_len, 2 * half)
    hb = tq // half
    n_halo = sub_len // half
    view = qkv.reshape(batch, sub_len, dil * n)
    base = group * N_QKV

    def q_map(b, r, l):
        return (b, l, r * nblk + base)

    def cur_map(kind):
        return lambda b, r, l: (b, l, r * nblk + base + kind)

    def prev_map(kind):
        return lambda b, r, l: (b, jnp.maximum(l * hb - 1, 0), r * nblk + base + kind)

    def next_map(kind):
        return lambda b, r, l: (b, jnp.minimum((l + 1) * hb, n_halo - 1), r * nblk + base + kind)

    cur = lambda kind: pl.BlockSpec((None, tq, aw), cur_map(kind))
    prev = lambda kind: pl.BlockSpec((None, half, aw), prev_map(kind))
    nxt = lambda kind: pl.BlockSpec((None, half, aw), next_map(kind))

    kern = functools.partial(_band_attn_kernel, heads=heads, sub_len=sub_len, half=half)
    o, lse = pl.pallas_call(
        kern,
        grid=(batch, dil, sub_len // tq),
        in_specs=[pl.BlockSpec((None, tq, aw), q_map),
                  prev(1), cur(1), nxt(1), prev(2), cur(2), nxt(2)],
        out_specs=[
            pl.BlockSpec((None, tq, aw), lambda b, r, l: (b, l, r)),
            pl.BlockSpec((None, tq, LANES), lambda b, r, l: (b, l, r)),
        ],
        out_shape=[
            jax.ShapeDtypeStruct((batch, sub_len, dil * aw), F32),
            jax.ShapeDtypeStruct((batch, sub_len, dil * LANES), F32),
        ],
        scratch_shapes=[pltpu.VMEM((tq + 2 * half, aw), BF16),
                        pltpu.VMEM((tq + 2 * half, aw), BF16)],
        compiler_params=_params("parallel", "parallel", "parallel"),
        name=f"band_attn_g{group}",
    )(view, view, view, view, view, view, view)
    return o.reshape(m, aw), lse.reshape(m, LANES)


def _attn_out_kernel(*refs, n_groups, heads, alpha, nc):
    o_refs = refs[:n_groups]
    l_refs = refs[n_groups:2 * n_groups]
    x_ref, wo_ref, g_ref, b_ref, out_ref, a_ref = refs[2 * n_groups:]
    d = out_ref.shape[1]
    ls = [r[...] for r in l_refs]
    mx = functools.reduce(jnp.maximum, ls)
    es = [jnp.exp(l - mx) for l in ls]
    inv = 1.0 / functools.reduce(jnp.add, es)
    ws = [e * inv for e in es]
    for hd in range(heads):
        cols = slice(hd * HEAD_DIM, (hd + 1) * HEAD_DIM)
        acc = ws[0][:, hd:hd + 1] * o_refs[0][:, cols]
        for g in range(1, n_groups):
            acc = acc + ws[g][:, hd:hd + 1] * o_refs[g][:, cols]
        a_ref[:, cols] = acc.astype(BF16)
    for n in range(0, d, nc):
        out_ref[:, n:n + nc] = alpha * x_ref[:, n:n + nc] + _mm(a_ref[...], wo_ref[:, n:n + nc])
    out_ref[...] = _layer_norm(out_ref[...], g_ref[...], b_ref[...])


def _attn_out(os_, lses, x, w_o, ln_g, ln_b, alpha, heads):
    m, d = x.shape
    aw = heads * HEAD_DIM
    ng = len(os_)
    tm = _row_tile(m, 256)
    kern = functools.partial(_attn_out_kernel, n_groups=ng, heads=heads, alpha=alpha,
                             nc=_col_chunk(d))
    row = lambda w: pl.BlockSpec((tm, w), lambda i: (i, 0))
    return pl.pallas_call(
        kern,
        grid=(m // tm,),
        in_specs=[row(aw)] * ng + [row(LANES)] * ng + [
            row(d), _resident(w_o.shape), _resident(ln_g.shape), _resident(ln_b.shape)],
        out_specs=row(d),
        out_shape=jax.ShapeDtypeStruct((m, d), F32),
        scratch_shapes=[pltpu.VMEM((tm, aw), BF16)],
        compiler_params=_params("parallel"),
        name="attn_out",
    )(*os_, *lses, x, w_o, ln_g, ln_b)


def _trunk(x, p, w):
    batch, seq, d = x.shape
    depth = p.shape[0]
    alpha = (2.0 * depth) ** 0.25
    m = batch * seq
    heads = w["w_o_b"].shape[1] // HEAD_DIM
    tables = _rope_tables(seq)
    h = x.reshape(m, d)
    for i in range(depth):
        j = i // 2
        ln_g = w["ln_mix_g"][i][None]
        ln_b = w["ln_mix_b"][i][None]
        if i % 2 == 0:
            u, v = _gmlp_in(h, w["w_in_a"][j], w["ln_v_g"][j][None], w["ln_v_b"][j][None])
            h = _gmlp_out(u, v, h, w["w_s_a"][j], w["sgu_bias"][j], w["w_o_a"][j], ln_g, ln_b, alpha)
        else:
            qkv = _qkv(h, w["w_qkv_b"][j], tables, seq, heads)
            os_, lses = [], []
            for g, (window, dil) in enumerate(DILATION_PAIRS):
                o, lse = _band_attn(qkv, g, dil, window // (2 * dil), batch, seq, heads)
                os_.append(o)
                lses.append(lse)
            h = _attn_out(os_, lses, h, w["w_o_b"][j], ln_g, ln_b, alpha, heads)
        h = _ffn(h, w["w_ffn_gu"][i], w["w_ffn_down"][i],
                 w["ln_ffn_g"][i][None], w["ln_ffn_b"][i][None], alpha)
        h = _ple(h, p[i].reshape(m, -1), w["w_ple_gate"][i], w["w_ple_proj"][i])
    return h.reshape(batch, seq, d)


def kernel(x_prompt, x_sample, p_prompt, p_sample, w_in_a, ln_v_g, ln_v_b, w_s_a, b_s_a, w_o_a, w_qkv_b, w_o_b, ln_mix_g, ln_mix_b, w_ffn_gu, w_ffn_down, ln_ffn_g, ln_ffn_b, w_ple_gate, w_ple_proj):
    group_dim = w_in_a.shape[2] // 2 // w_s_a.shape[1]
    assert group_dim == GROUP_DIM and w_s_a.shape[2] == CHUNK
    sgu_bias = jnp.repeat(jnp.swapaxes(b_s_a, 1, 2), group_dim, axis=2)
    w = dict(
        w_in_a=w_in_a.astype(BF16), ln_v_g=ln_v_g, ln_v_b=ln_v_b,
        w_s_a=w_s_a.astype(BF16), sgu_bias=sgu_bias, w_o_a=w_o_a.astype(BF16),
        w_qkv_b=w_qkv_b.astype(BF16), w_o_b=w_o_b.astype(BF16),
        ln_mix_g=ln_mix_g, ln_mix_b=ln_mix_b,
        w_ffn_gu=w_ffn_gu.astype(BF16), w_ffn_down=w_ffn_down.astype(BF16),
        ln_ffn_g=ln_ffn_g, ln_ffn_b=ln_ffn_b,
        w_ple_gate=w_ple_gate.astype(BF16), w_ple_proj=w_ple_proj.astype(BF16),
    )
    return (_trunk(x_prompt, p_prompt, w), _trunk(x_sample, p_sample, w))
```

```python
import functools
import math

import jax
import jax.numpy as jnp
from jax import lax
from jax.experimental import pallas as pl
from jax.experimental.pallas import tpu as pltpu

F32 = jnp.float32
BF16 = jnp.bfloat16

LN_EPS = 1e-5
ROPE_THETA = 500000.0
NEG_BIG = -1e30
HEAD_DIM = 128
ROT_DIM = HEAD_DIM // 4
CHUNK = 128
GROUP_DIM = 128
DILATION_PAIRS = ((128, 1), (512, 4), (2048, 16))
N_QKV = 3

V7X_VMEM_LIMIT_BYTES = 56 * 1024 * 1024
LANES = 128
MAX_ROW_STRIDE = 8


def _params(*semantics):
    return pltpu.CompilerParams(dimension_semantics=semantics,
                                vmem_limit_bytes=V7X_VMEM_LIMIT_BYTES)


def _resident(shape):
    nd = len(shape)
    return pl.BlockSpec(shape, lambda *_: (0,) * nd, pipeline_mode=pl.Buffered(1))


def _row_tile(m, pref):
    t = min(m, pref)
    assert m % t == 0, (m, t)
    return t


def _col_chunk(n, pref=512):
    c = min(n, pref)
    assert n % c == 0, (n, c)
    return c


def _layer_norm(y, g, b):
    mu = jnp.mean(y, axis=-1, keepdims=True)
    yc = y - mu
    var = jnp.mean(yc * yc, axis=-1, keepdims=True)
    return yc * lax.rsqrt(var + LN_EPS) * g + b


def _gelu(h):
    return 0.5 * h * (1.0 + lax.erf(h * math.sqrt(0.5)))


def _sigmoid(z):
    return 1.0 / (1.0 + jnp.exp(-z))


def _mm(a, b):
    return jnp.dot(a, b, preferred_element_type=F32)


def _gmlp_in_kernel(x_ref, w_ref, g_ref, b_ref, u_ref, v_ref, hv_ref, *, width, nc):
    xb = x_ref[...].astype(BF16)
    for n in range(0, width, nc):
        u_ref[:, n:n + nc] = _gelu(_mm(xb, w_ref[:, n:n + nc]))
    for n in range(0, width, nc):
        hv_ref[:, n:n + nc] = _gelu(_mm(xb, w_ref[:, width + n:width + n + nc]))
    v_ref[...] = _layer_norm(hv_ref[...], g_ref[...], b_ref[...]).astype(BF16)


def _gmlp_in(x, w_in, ln_g, ln_b):
    m, d = x.shape
    width = w_in.shape[1] // 2
    tm = _row_tile(m, 256)
    kern = functools.partial(_gmlp_in_kernel, width=width, nc=_col_chunk(width))
    return pl.pallas_call(
        kern,
        grid=(m // tm,),
        in_specs=[
            pl.BlockSpec((tm, d), lambda i: (i, 0)),
            _resident(w_in.shape),
            _resident(ln_g.shape),
            _resident(ln_b.shape),
        ],
        out_specs=[
            pl.BlockSpec((tm, width), lambda i: (i, 0)),
            pl.BlockSpec((tm, width), lambda i: (i, 0)),
        ],
        out_shape=[
            jax.ShapeDtypeStruct((m, width), F32),
            jax.ShapeDtypeStruct((m, width), BF16),
        ],
        scratch_shapes=[pltpu.VMEM((tm, width), F32)],
        compiler_params=_params("parallel"),
        name="gmlp_in",
    )(x, w_in, ln_g, ln_b)


def _gmlp_out_kernel(u_ref, v_ref, x_ref, ws_ref, bias_ref, wo_ref, g_ref, b_ref,
                     o_ref, t_ref, *, groups, alpha, nc):
    tm, d = o_ref.shape
    for c in range(tm // CHUNK):
        rows = slice(c * CHUNK, (c + 1) * CHUNK)
        for g in range(groups):
            cols = slice(g * GROUP_DIM, (g + 1) * GROUP_DIM)
            sv = _mm(ws_ref[g], v_ref[rows, cols]) + bias_ref[:, cols]
            t_ref[rows, cols] = (u_ref[rows, cols] * sv).astype(BF16)
    for n in range(0, d, nc):
        o_ref[:, n:n + nc] = alpha * x_ref[:, n:n + nc] + _mm(t_ref[...], wo_ref[:, n:n + nc])
    o_ref[...] = _layer_norm(o_ref[...], g_ref[...], b_ref[...])


def _gmlp_out(u, v, x, w_s, bias, w_o, ln_g, ln_b, alpha):
    m, d = x.shape
    width = u.shape[1]
    groups = w_s.shape[0]
    tm = _row_tile(m, 256)
    kern = functools.partial(_gmlp_out_kernel, groups=groups, alpha=alpha, nc=_col_chunk(d))
    return pl.pallas_call(
        kern,
        grid=(m // tm,),
        in_specs=[
            pl.BlockSpec((tm, width), lambda i: (i, 0)),
            pl.BlockSpec((tm, width), lambda i: (i, 0)),
            pl.BlockSpec((tm, d), lambda i: (i, 0)),
            _resident(w_s.shape),
            _resident(bias.shape),
            _resident(w_o.shape),
            _resident(ln_g.shape),
            _resident(ln_b.shape),
        ],
        out_specs=pl.BlockSpec((tm, d), lambda i: (i, 0)),
        out_shape=jax.ShapeDtypeStruct((m, d), F32),
        scratch_shapes=[pltpu.VMEM((tm, width), BF16)],
        compiler_params=_params("parallel"),
        name="gmlp_out",
    )(u, v, x, w_s, bias, w_o, ln_g, ln_b)


def _ffn_kernel(x_ref, wg_ref, wu_ref, wd_ref, g_ref, b_ref, o_ref, xb_ref, acc_ref, *, alpha, nc):
    f = pl.program_id(1)

    @pl.when(f == 0)
    def _():
        xb_ref[...] = x_ref[...].astype(BF16)
        acc_ref[...] = jnp.zeros_like(acc_ref)

    xb = xb_ref[...]
    gate = _mm(xb, wg_ref[...])
    up = _mm(xb, wu_ref[...])
    act = (gate * _sigmoid(gate) * up).astype(BF16)
    for n in range(0, acc_ref.shape[1], nc):
        acc_ref[:, n:n + nc] += _mm(act, wd_ref[:, n:n + nc])

    @pl.when(f == pl.num_programs(1) - 1)
    def _():
        y = alpha * x_ref[...] + acc_ref[...]
        o_ref[...] = _layer_norm(y, g_ref[...], b_ref[...])


def _ffn(x, w_gu, w_down, ln_g, ln_b, alpha):
    m, d = x.shape
    dff = w_down.shape[0]
    tm = _row_tile(m, 512)
    tf = _col_chunk(dff, 512)
    nf = dff // tf
    kern = functools.partial(_ffn_kernel, alpha=alpha, nc=_col_chunk(d))
    return pl.pallas_call(
        kern,
        grid=(m // tm, nf),
        in_specs=[
            pl.BlockSpec((tm, d), lambda i, f: (i, 0)),
            pl.BlockSpec((d, tf), lambda i, f: (0, f)),
            pl.BlockSpec((d, tf), lambda i, f: (0, nf + f)),
            pl.BlockSpec((tf, d), lambda i, f: (f, 0)),
            _resident(ln_g.shape),
            _resident(ln_b.shape),
        ],
        out_specs=pl.BlockSpec((tm, d), lambda i, f: (i, 0)),
        out_shape=jax.ShapeDtypeStruct((m, d), F32),
        scratch_shapes=[pltpu.VMEM((tm, d), BF16), pltpu.VMEM((tm, d), F32)],
        compiler_params=_params("parallel", "arbitrary"),
        name="ffn",
    )(x, w_gu, w_gu, w_down, ln_g, ln_b)


def _ple_kernel(x_ref, p_ref, wg_ref, wp_ref, o_ref, *, nc):
    d = o_ref.shape[1]
    xb = x_ref[...].astype(BF16)
    pb = p_ref[...].astype(BF16)
    for n in range(0, d, nc):
        gate = _sigmoid(_mm(xb, wg_ref[:, n:n + nc]))
        o_ref[:, n:n + nc] = x_ref[:, n:n + nc] + gate * _mm(pb, wp_ref[:, n:n + nc])


def _ple(x, p, w_gate, w_proj):
    m, d = x.shape
    pd = p.shape[1]
    tm = _row_tile(m, 512)
    kern = functools.partial(_ple_kernel, nc=_col_chunk(d))
    return pl.pallas_call(
        kern,
        grid=(m // tm,),
        in_specs=[
            pl.BlockSpec((tm, d), lambda i: (i, 0)),
            pl.BlockSpec((tm, pd), lambda i: (i, 0)),
            _resident(w_gate.shape),
            _resident(w_proj.shape),
        ],
        out_specs=pl.BlockSpec((tm, d), lambda i: (i, 0)),
        out_shape=jax.ShapeDtypeStruct((m, d), F32),
        compiler_params=_params("parallel"),
        name="ple",
    )(x, p, w_gate, w_proj)


def _rope_tables(seq):
    half = ROT_DIM // 2
    inv_freq = ROPE_THETA ** (-jnp.arange(0, ROT_DIM, 2, dtype=F32) / ROT_DIM)
    ang = jnp.arange(seq, dtype=F32)[:, None] * inv_freq[None, :]
    cos, sin = jnp.cos(ang), jnp.sin(ang)
    rest = HEAD_DIM - ROT_DIM
    c = jnp.concatenate([cos, cos, jnp.ones((seq, rest), F32)], axis=1)
    sa = jnp.concatenate([-sin, jnp.zeros((seq, HEAD_DIM - half), F32)], axis=1)
    sb = jnp.concatenate([jnp.zeros((seq, half), F32), sin, jnp.zeros((seq, rest), F32)], axis=1)
    scale = HEAD_DIM ** -0.5 * math.log2(math.e)
    return tuple(jnp.stack([t * scale, t]) for t in (c, sa, sb))


def _tile_deinterleave(tab, tile, dil):
    k, seq, w = tab.shape
    return tab.reshape(k, seq // tile, tile // dil, dil, w).swapaxes(2, 3).reshape(k, seq, w)


def _qkv_kernel(x_ref, w_ref, c_ref, sa_ref, sb_ref, o_ref, xb_ref, *stage, heads, dil, nc):
    kind = pl.program_id(1)
    tm = x_ref.shape[0]
    lt = tm // dil
    aw = heads * HEAD_DIM

    @pl.when(kind == 0)
    def _():
        if dil == 1:
            xb_ref[...] = x_ref[...].astype(BF16)
        else:
            s1 = dil if dil <= MAX_ROW_STRIDE else MAX_ROW_STRIDE // 2
            s2 = dil // s1
            assert s1 * s2 == dil and s2 <= MAX_ROW_STRIDE
            xs_ref, xt_ref = stage
            for c in range(x_ref.shape[1] // LANES):
                cols = slice(c * LANES, (c + 1) * LANES)
                xs_ref[...] = x_ref[:, cols]
                if s2 == 1:
                    for r in range(dil):
                        xb_ref[r * lt:(r + 1) * lt, cols] = (
                            xs_ref[pl.ds(r, lt, stride=dil), :].astype(BF16))
                    continue
                l1 = tm // s1
                for r1 in range(s1):
                    xt_ref[r1 * l1:(r1 + 1) * l1, :] = xs_ref[pl.ds(r1, l1, stride=s1), :]
                for r1 in range(s1):
                    for r2 in range(s2):
                        r = r2 * s1 + r1
                        xb_ref[r * lt:(r + 1) * lt, cols] = (
                            xt_ref[pl.ds(r1 * l1 + r2, lt, stride=s2), :].astype(BF16))

    def project(rotate):
        half = ROT_DIM // 2
        for n in range(0, aw, nc):
            h = _mm(xb_ref[...], w_ref[:, n:n + nc])
            for hd in range(nc // HEAD_DIM):
                t = h[:, hd * HEAD_DIM:(hd + 1) * HEAD_DIM]
                if rotate:
                    t = (t * c_ref[...] + pltpu.roll(t, HEAD_DIM - half, 1) * sa_ref[...]
                         + pltpu.roll(t, half, 1) * sb_ref[...])
                tb = t.astype(BF16)
                col = n + hd * HEAD_DIM
                for r in range(dil):
                    o_ref[:, r * aw + col:r * aw + col + HEAD_DIM] = tb[r * lt:(r + 1) * lt]

    @pl.when(kind == 2)
    def _():
        project(False)

    @pl.when(kind != 2)
    def _():
        project(True)


def _qkv(x, w_qkv, tables, group, dil, seq, heads):
    m, d = x.shape
    aw = heads * HEAD_DIM
    tm = _row_tile(seq, 1024)
    spt = seq // tm
    nc = min(aw, 2 * HEAD_DIM)
    kern = functools.partial(_qkv_kernel, heads=heads, dil=dil, nc=nc)
    tab_spec = pl.BlockSpec((None, tm, HEAD_DIM), lambda i, k: (jnp.minimum(k, 1), i % spt, 0))
    tabs = [_tile_deinterleave(t, tm, dil) for t in tables]
    return pl.pallas_call(
        kern,
        grid=(m // tm, N_QKV),
        in_specs=[
            pl.BlockSpec((tm, d), lambda i, k: (i, 0)),
            pl.BlockSpec((d, aw), lambda i, k: (0, group * N_QKV + k)),
            tab_spec, tab_spec, tab_spec,
        ],
        out_specs=pl.BlockSpec((tm // dil, dil * aw), lambda i, k: (i, k)),
        out_shape=jax.ShapeDtypeStruct((m // dil, N_QKV * dil * aw), BF16),
        scratch_shapes=[pltpu.VMEM((tm, d), BF16)]
        + ([pltpu.VMEM((tm, LANES), F32)] * 2 if dil > 1 else []),
        compiler_params=_params("parallel", "arbitrary"),
        name=f"qkv_rope_g{group}",
    )(x, w_qkv, *tabs)


def _band_attn_kernel(q_ref, kp_ref, kc_ref, kn_ref, vp_ref, vc_ref, vn_ref,
                      o_ref, lse_ref, kcat_ref, vcat_ref, *, heads, sub_len, half):
    tq = q_ref.shape[0]
    nk = tq + 2 * half
    blk = pl.program_id(2)
    kcat_ref[0:half] = kp_ref[...]
    kcat_ref[half:half + tq] = kc_ref[...]
    kcat_ref[half + tq:nk] = kn_ref[...]
    vcat_ref[0:half] = vp_ref[...]
    vcat_ref[half:half + tq] = vc_ref[...]
    vcat_ref[half + tq:nk] = vn_ref[...]

    sq = min(tq, 2 * half)
    wk = sq + 2 * half
    row = lax.broadcasted_iota(jnp.int32, (sq, wk), 0)
    col = lax.broadcasted_iota(jnp.int32, (sq, wk), 1)
    rel = col - row
    band = (rel >= 0) & (rel <= 2 * half)
    lane = lax.broadcasted_iota(jnp.int32, (sq, LANES), 1)
    for j in range(tq // sq):
        key = blk * tq + j * sq - half + col
        mask = band & (key >= 0) & (key < sub_len)
        rows = slice(j * sq, (j + 1) * sq)
        keys = slice(j * sq, j * sq + wk)
        lse_all = jnp.zeros((sq, LANES), F32)
        for hd in range(heads):
            cols = slice(hd * HEAD_DIM, (hd + 1) * HEAD_DIM)
            s = lax.dot_general(q_ref[rows, cols], kcat_ref[keys, cols],
                                (((1,), (1,)), ((), ())), preferred_element_type=F32)
            s = jnp.where(mask, s, NEG_BIG)
            mx = jnp.max(s, axis=-1, keepdims=True)
            p = jnp.exp2(s - mx)
            den = jnp.sum(p, axis=-1, keepdims=True)
            o = _mm(p.astype(BF16), vcat_ref[keys, cols])
            o_ref[rows, cols] = o / den
            lse_all = jnp.where(lane == hd, mx * math.log(2.0) + jnp.log(den), lse_all)
        lse_ref[rows, :] = lse_all


def _band_attn(qkv, group, dil, half, batch, seq, heads):
    aw = heads * HEAD_DIM
    sub_len = seq // dil
    tq = _row_tile(sub_len, 4 * half)
    hb = tq // half
    n_halo = sub_len // half
    view = qkv.reshape(batch, sub_len, N_QKV * dil * aw)

    def q_map(b, r, l):
        return (b, l, r)

    def cur_map(kind):
        return lambda b, r, l: (b, l, kind * dil + r)

    def prev_map(kind):
        return lambda b, r, l: (b, jnp.maximum(l * hb - 1, 0), kind * dil + r)

    def next_map(kind):
        return lambda b, r, l: (b, jnp.minimum((l + 1) * hb, n_halo - 1), kind * dil + r)

    cur = lambda kind: pl.BlockSpec((None, tq, aw), cur_map(kind))
    prev = lambda kind: pl.BlockSpec((None, half, aw), prev_map(kind))
    nxt = lambda kind: pl.BlockSpec((None, half, aw), next_map(kind))

    kern = functools.partial(_band_attn_kernel, heads=heads, sub_len=sub_len, half=half)
    o, lse = pl.pallas_call(
        kern,
        grid=(batch, dil, sub_len // tq),
        in_specs=[pl.BlockSpec((None, tq, aw), q_map),
                  prev(1), cur(1), nxt(1), prev(2), cur(2), nxt(2)],
        out_specs=[
            pl.BlockSpec((None, tq, aw), lambda b, r, l: (b, l, r)),
            pl.BlockSpec((None, tq, LANES), lambda b, r, l: (b, l, r)),
        ],
        out_shape=[
            jax.ShapeDtypeStruct((batch, sub_len, dil * aw), F32),
            jax.ShapeDtypeStruct((batch, sub_len, dil * LANES), F32),
        ],
        scratch_shapes=[pltpu.VMEM((tq + 2 * half, aw), BF16),
                        pltpu.VMEM((tq + 2 * half, aw), BF16)],
        compiler_params=_params("parallel", "parallel", "parallel"),
        name=f"band_attn_g{group}",
    )(view, view, view, view, view, view, view)
    return (o.reshape(batch * sub_len, dil * aw), lse.reshape(batch * sub_len, dil * LANES))


def _attn_out_kernel(*refs, dils, heads, alpha, nc):
    n_groups = len(dils)
    o_refs = list(refs[:n_groups])
    l_refs = list(refs[n_groups:2 * n_groups])
    x_ref, wo_ref, g_ref, b_ref, out_ref, a_ref = refs[2 * n_groups:2 * n_groups + 6]
    stage = refs[2 * n_groups + 6:]
    tm, d = out_ref.shape
    aw = heads * HEAD_DIM
    head_o = [[None] * heads for _ in dils]
    s = 0
    for g, dil in enumerate(dils):
        if dil == 1:
            for hd in range(heads):
                head_o[g][hd] = o_refs[g].at[:, hd * HEAD_DIM:(hd + 1) * HEAD_DIM]
            continue
        on_ref, ln_ref = stage[s], stage[s + 1]
        s += 2
        lt = tm // dil
        for r in range(dil):
            ln_ref[pl.ds(r, lt, stride=dil), :] = l_refs[g][:, r * LANES:(r + 1) * LANES]
            for hd in range(heads):
                col = r * aw + hd * HEAD_DIM
                on_ref[pl.ds(hd * tm + r, lt, stride=dil), :] = o_refs[g][:, col:col + HEAD_DIM]
        for hd in range(heads):
            head_o[g][hd] = on_ref.at[hd * tm:(hd + 1) * tm, :]
        l_refs[g] = ln_ref
    ls = [r[...] for r in l_refs]
    mx = functools.reduce(jnp.maximum, ls)
    es = [jnp.exp(l - mx) for l in ls]
    inv = 1.0 / functools.reduce(jnp.add, es)
    ws = [e * inv for e in es]
    for hd in range(heads):
        cols = slice(hd * HEAD_DIM, (hd + 1) * HEAD_DIM)
        acc = ws[0][:, hd:hd + 1] * head_o[0][hd][...]
        for g in range(1, n_groups):
            acc = acc + ws[g][:, hd:hd + 1] * head_o[g][hd][...]
        a_ref[:, cols] = acc.astype(BF16)
    for n in range(0, d, nc):
        out_ref[:, n:n + nc] = alpha * x_ref[:, n:n + nc] + _mm(a_ref[...], wo_ref[:, n:n + nc])
    out_ref[...] = _layer_norm(out_ref[...], g_ref[...], b_ref[...])


def _attn_out(os_, lses, dils, x, w_o, ln_g, ln_b, alpha, heads):
    m, d = x.shape
    aw = heads * HEAD_DIM
    tm = _row_tile(m, 256)
    kern = functools.partial(_attn_out_kernel, dils=tuple(dils), heads=heads, alpha=alpha,
                             nc=_col_chunk(d))
    row = lambda w: pl.BlockSpec((tm, w), lambda i: (i, 0))
    packed = lambda w: [pl.BlockSpec((tm // dil, dil * w), lambda i: (i, 0)) for dil in dils]
    stage = []
    for dil in dils:
        if dil > 1:
            stage += [pltpu.VMEM((heads * tm, LANES), F32), pltpu.VMEM((tm, LANES), F32)]
    return pl.pallas_call(
        kern,
        grid=(m // tm,),
        in_specs=packed(aw) + packed(LANES) + [
            row(d), _resident(w_o.shape), _resident(ln_g.shape), _resident(ln_b.shape)],
        out_specs=row(d),
        out_shape=jax.ShapeDtypeStruct((m, d), F32),
        scratch_shapes=[pltpu.VMEM((tm, aw), BF16)] + stage,
        compiler_params=_params("parallel"),
        name="attn_out",
    )(*os_, *lses, x, w_o, ln_g, ln_b)


def _trunk(x, p, w):
    batch, seq, d = x.shape
    depth = p.shape[0]
    alpha = (2.0 * depth) ** 0.25
    m = batch * seq
    heads = w["w_o_b"].shape[1] // HEAD_DIM
    tables = _rope_tables(seq)
    h = x.reshape(m, d)
    for i in range(depth):
        j = i // 2
        ln_g = w["ln_mix_g"][i][None]
        ln_b = w["ln_mix_b"][i][None]
        if i % 2 == 0:
            u, v = _gmlp_in(h, w["w_in_a"][j], w["ln_v_g"][j][None], w["ln_v_b"][j][None])
            h = _gmlp_out(u, v, h, w["w_s_a"][j], w["sgu_bias"][j], w["w_o_a"][j], ln_g, ln_b, alpha)
        else:
            os_, lses = [], []
            for g, (window, dil) in enumerate(DILATION_PAIRS):
                qkv = _qkv(h, w["w_qkv_b"][j], tables, g, dil, seq, heads)
                o, lse = _band_attn(qkv, g, dil, window // (2 * dil), batch, seq, heads)
                os_.append(o)
                lses.append(lse)
            dils = [dil for _, dil in DILATION_PAIRS]
            h = _attn_out(os_, lses, dils, h, w["w_o_b"][j], ln_g, ln_b, alpha, heads)
        h = _ffn(h, w["w_ffn_gu"][i], w["w_ffn_down"][i],
                 w["ln_ffn_g"][i][None], w["ln_ffn_b"][i][None], alpha)
        h = _ple(h, p[i].reshape(m, -1), w["w_ple_gate"][i], w["w_ple_proj"][i])
    return h.reshape(batch, seq, d)


def kernel(x_prompt, x_sample, p_prompt, p_sample, w_in_a, ln_v_g, ln_v_b, w_s_a, b_s_a, w_o_a, w_qkv_b, w_o_b, ln_mix_g, ln_mix_b, w_ffn_gu, w_ffn_down, ln_ffn_g, ln_ffn_b, w_ple_gate, w_ple_proj):
    group_dim = w_in_a.shape[2] // 2 // w_s_a.shape[1]
    assert group_dim == GROUP_DIM and w_s_a.shape[2] == CHUNK
    sgu_bias = jnp.repeat(jnp.swapaxes(b_s_a, 1, 2), group_dim, axis=2)
    w = dict(
        w_in_a=w_in_a.astype(BF16), ln_v_g=ln_v_g, ln_v_b=ln_v_b,
        w_s_a=w_s_a.astype(BF16), sgu_bias=sgu_bias, w_o_a=w_o_a.astype(BF16),
        w_qkv_b=w_qkv_b.astype(BF16), w_o_b=w_o_b.astype(BF16),
        ln_mix_g=ln_mix_g, ln_mix_b=ln_mix_b,
        w_ffn_gu=w_ffn_gu.astype(BF16), w_ffn_down=w_ffn_down.astype(BF16),
        ln_ffn_g=ln_ffn_g, ln_ffn_b=ln_ffn_b,
        w_ple_gate=w_ple_gate.astype(BF16), w_ple_proj=w_ple_proj.astype(BF16),
    )
    return (_trunk(x_prompt, p_prompt, w), _trunk(x_sample, p_sample, w))
```

```python
import functools
import math

import jax
import jax.numpy as jnp
from jax import lax
from jax.experimental import pallas as pl
from jax.experimental.pallas import tpu as pltpu

F32 = jnp.float32
BF16 = jnp.bfloat16

LN_EPS = 1e-5
ROPE_THETA = 500000.0
NEG_BIG = -1e30
HEAD_DIM = 128
ROT_DIM = HEAD_DIM // 4
CHUNK = 128
GROUP_DIM = 128
DILATION_PAIRS = ((128, 1), (512, 4), (2048, 16))
N_QKV = 3

V7X_VMEM_LIMIT_BYTES = 56 * 1024 * 1024
LANES = 128
ROW_CHUNK = 128
MAX_ROW_STRIDE = 8


def _params(*semantics):
    return pltpu.CompilerParams(dimension_semantics=semantics,
                                vmem_limit_bytes=V7X_VMEM_LIMIT_BYTES)


def _resident(shape):
    nd = len(shape)
    return pl.BlockSpec(shape, lambda *_: (0,) * nd, pipeline_mode=pl.Buffered(1))


def _row_tile(m, pref):
    t = min(m, pref)
    assert m % t == 0, (m, t)
    return t


def _col_chunk(n, pref=512):
    c = min(n, pref)
    assert n % c == 0, (n, c)
    return c


def _layer_norm(y, g, b):
    mu = jnp.mean(y, axis=-1, keepdims=True)
    yc = y - mu
    var = jnp.mean(yc * yc, axis=-1, keepdims=True)
    return yc * lax.rsqrt(var + LN_EPS) * g + b


def _gelu(h):
    return 0.5 * h * (1.0 + lax.erf(h * math.sqrt(0.5)))


def _sigmoid(z):
    return 1.0 / (1.0 + jnp.exp(-z))


def _mm(a, b):
    return jnp.dot(a, b, preferred_element_type=F32)


def _gmlp_in_kernel(x_ref, w_ref, g_ref, b_ref, u_ref, v_ref, xb_ref, *, width, nc):
    tm = x_ref.shape[0]
    xb_ref[...] = x_ref[...].astype(BF16)
    for r in range(0, tm, ROW_CHUNK):
        rows = slice(r, r + ROW_CHUNK)
        hv = _gelu(_mm(xb_ref[rows, :], w_ref[:, width:]))
        v_ref[rows, :] = _layer_norm(hv, g_ref[...], b_ref[...]).astype(BF16)
    for n in range(0, width, nc):
        u_ref[:, n:n + nc] = _gelu(_mm(xb_ref[...], w_ref[:, n:n + nc]))


def _gmlp_in(x, w_in, ln_g, ln_b):
    m, d = x.shape
    width = w_in.shape[1] // 2
    tm = _row_tile(m, 512)
    kern = functools.partial(_gmlp_in_kernel, width=width, nc=_col_chunk(width))
    return pl.pallas_call(
        kern,
        grid=(m // tm,),
        in_specs=[
            pl.BlockSpec((tm, d), lambda i: (i, 0)),
            _resident(w_in.shape),
            _resident(ln_g.shape),
            _resident(ln_b.shape),
        ],
        out_specs=[
            pl.BlockSpec((tm, width), lambda i: (i, 0)),
            pl.BlockSpec((tm, width), lambda i: (i, 0)),
        ],
        out_shape=[
            jax.ShapeDtypeStruct((m, width), F32),
            jax.ShapeDtypeStruct((m, width), BF16),
        ],
        scratch_shapes=[pltpu.VMEM((tm, d), BF16)],
        compiler_params=_params("parallel"),
        name="gmlp_in",
    )(x, w_in, ln_g, ln_b)


def _gmlp_out_kernel(u_ref, v_ref, x_ref, ws_ref, bias_ref, wo_ref, g_ref, b_ref,
                     o_ref, t_ref, *, groups, alpha):
    tm, d = o_ref.shape
    for c in range(tm // CHUNK):
        rows = slice(c * CHUNK, (c + 1) * CHUNK)
        for g in range(groups):
            cols = slice(g * GROUP_DIM, (g + 1) * GROUP_DIM)
            sv = _mm(ws_ref[g], v_ref[rows, cols]) + bias_ref[:, cols]
            t_ref[rows, cols] = (u_ref[rows, cols] * sv).astype(BF16)
        y = alpha * x_ref[rows, :] + _mm(t_ref[rows, :], wo_ref[...])
        o_ref[rows, :] = _layer_norm(y, g_ref[...], b_ref[...])


def _gmlp_out(u, v, x, w_s, bias, w_o, ln_g, ln_b, alpha):
    m, d = x.shape
    width = u.shape[1]
    groups = w_s.shape[0]
    tm = _row_tile(m, 512)
    kern = functools.partial(_gmlp_out_kernel, groups=groups, alpha=alpha)
    return pl.pallas_call(
        kern,
        grid=(m // tm,),
        in_specs=[
            pl.BlockSpec((tm, width), lambda i: (i, 0)),
            pl.BlockSpec((tm, width), lambda i: (i, 0)),
            pl.BlockSpec((tm, d), lambda i: (i, 0)),
            _resident(w_s.shape),
            _resident(bias.shape),
            _resident(w_o.shape),
            _resident(ln_g.shape),
            _resident(ln_b.shape),
        ],
        out_specs=pl.BlockSpec((tm, d), lambda i: (i, 0)),
        out_shape=jax.ShapeDtypeStruct((m, d), F32),
        scratch_shapes=[pltpu.VMEM((tm, width), BF16)],
        compiler_params=_params("parallel"),
        name="gmlp_out",
    )(u, v, x, w_s, bias, w_o, ln_g, ln_b)


def _ffn_kernel(x_ref, wg_ref, wu_ref, wd_ref, g_ref, b_ref, o_ref, xb_ref, acc_ref, *, alpha, nc):
    f = pl.program_id(1)

    @pl.when(f == 0)
    def _():
        xb_ref[...] = x_ref[...].astype(BF16)
        acc_ref[...] = jnp.zeros_like(acc_ref)

    xb = xb_ref[...]
    gate = _mm(xb, wg_ref[...])
    up = _mm(xb, wu_ref[...])
    act = (gate * _sigmoid(gate) * up).astype(BF16)
    for n in range(0, acc_ref.shape[1], nc):
        acc_ref[:, n:n + nc] += _mm(act, wd_ref[:, n:n + nc])

    @pl.when(f == pl.num_programs(1) - 1)
    def _():
        y = alpha * x_ref[...] + acc_ref[...]
        o_ref[...] = _layer_norm(y, g_ref[...], b_ref[...])


def _ffn(x, w_gu, w_down, ln_g, ln_b, alpha):
    m, d = x.shape
    dff = w_down.shape[0]
    tm = _row_tile(m, 512)
    tf = _col_chunk(dff, 512)
    nf = dff // tf
    kern = functools.partial(_ffn_kernel, alpha=alpha, nc=_col_chunk(d))
    return pl.pallas_call(
        kern,
        grid=(m // tm, nf),
        in_specs=[
            pl.BlockSpec((tm, d), lambda i, f: (i, 0)),
            pl.BlockSpec((d, tf), lambda i, f: (0, f)),
            pl.BlockSpec((d, tf), lambda i, f: (0, nf + f)),
            pl.BlockSpec((tf, d), lambda i, f: (f, 0)),
            _resident(ln_g.shape),
            _resident(ln_b.shape),
        ],
        out_specs=pl.BlockSpec((tm, d), lambda i, f: (i, 0)),
        out_shape=jax.ShapeDtypeStruct((m, d), F32),
        scratch_shapes=[pltpu.VMEM((tm, d), BF16), pltpu.VMEM((tm, d), F32)],
        compiler_params=_params("parallel", "arbitrary"),
        name="ffn",
    )(x, w_gu, w_gu, w_down, ln_g, ln_b)


def _ple_kernel(x_ref, p_ref, wg_ref, wp_ref, o_ref, *, nc):
    d = o_ref.shape[1]
    xb = x_ref[...].astype(BF16)
    pb = p_ref[...].astype(BF16)
    for n in range(0, d, nc):
        gate = _sigmoid(_mm(xb, wg_ref[:, n:n + nc]))
        o_ref[:, n:n + nc] = x_ref[:, n:n + nc] + gate * _mm(pb, wp_ref[:, n:n + nc])


def _ple(x, p, layer, w_gate, w_proj):
    m, d = x.shape
    pd = p.shape[2]
    tm = _row_tile(m, 512)
    kern = functools.partial(_ple_kernel, nc=_col_chunk(d))
    return pl.pallas_call(
        kern,
        grid=(m // tm,),
        in_specs=[
            pl.BlockSpec((tm, d), lambda i: (i, 0)),
            pl.BlockSpec((None, tm, pd), lambda i: (layer, i, 0)),
            _resident(w_gate.shape),
            _resident(w_proj.shape),
        ],
        out_specs=pl.BlockSpec((tm, d), lambda i: (i, 0)),
        out_shape=jax.ShapeDtypeStruct((m, d), F32),
        compiler_params=_params("parallel"),
        name="ple",
    )(x, p, w_gate, w_proj)


def _rope_tables(seq):
    half = ROT_DIM // 2
    inv_freq = ROPE_THETA ** (-jnp.arange(0, ROT_DIM, 2, dtype=F32) / ROT_DIM)
    ang = jnp.arange(seq, dtype=F32)[:, None] * inv_freq[None, :]
    cos, sin = jnp.cos(ang), jnp.sin(ang)
    rest = HEAD_DIM - ROT_DIM
    c = jnp.concatenate([cos, cos, jnp.ones((seq, rest), F32)], axis=1)
    sa = jnp.concatenate([-sin, jnp.zeros((seq, HEAD_DIM - half), F32)], axis=1)
    sb = jnp.concatenate([jnp.zeros((seq, half), F32), sin, jnp.zeros((seq, rest), F32)], axis=1)
    return c, sa, sb


def _tile_deinterleave(tab, tile, dil):
    if dil == 1:
        return tab
    seq, w = tab.shape
    return tab.reshape(seq // tile, tile // dil, dil, w).swapaxes(1, 2).reshape(seq, w)


def _qkv_kernel(x_ref, w_ref, c_ref, sa_ref, sb_ref, o_ref, xb_ref, *stage, heads, dil, nc):
    tm = x_ref.shape[0]
    lt = tm // dil
    aw = heads * HEAD_DIM

    if dil == 1:
        xb_ref[...] = x_ref[...].astype(BF16)
    else:
        s1 = dil if dil <= MAX_ROW_STRIDE else MAX_ROW_STRIDE // 2
        s2 = dil // s1
        assert s1 * s2 == dil and s2 <= MAX_ROW_STRIDE
        xs_ref, xt_ref = stage
        for c in range(x_ref.shape[1] // LANES):
            cols = slice(c * LANES, (c + 1) * LANES)
            xs_ref[...] = x_ref[:, cols]
            if s2 == 1:
                for r in range(dil):
                    xb_ref[r * lt:(r + 1) * lt, cols] = (
                        xs_ref[pl.ds(r, lt, stride=dil), :].astype(BF16))
                continue
            l1 = tm // s1
            for r1 in range(s1):
                xt_ref[r1 * l1:(r1 + 1) * l1, :] = xs_ref[pl.ds(r1, l1, stride=s1), :]
            for r1 in range(s1):
                for r2 in range(s2):
                    r = r2 * s1 + r1
                    xb_ref[r * lt:(r + 1) * lt, cols] = (
                        xt_ref[pl.ds(r1 * l1 + r2, lt, stride=s2), :].astype(BF16))

    half = ROT_DIM // 2
    for kind in range(N_QKV):
        for n in range(0, aw, nc):
            h = _mm(xb_ref[...], w_ref[:, kind * aw + n:kind * aw + n + nc])
            for hd in range(nc // HEAD_DIM):
                t = h[:, hd * HEAD_DIM:(hd + 1) * HEAD_DIM]
                if kind < 2:
                    t = (t * c_ref[...] + pltpu.roll(t, HEAD_DIM - half, 1) * sa_ref[...]
                         + pltpu.roll(t, half, 1) * sb_ref[...])
                tb = t.astype(BF16)
                col = n + hd * HEAD_DIM
                for r in range(dil):
                    base = (kind * dil + r) * aw + col
                    o_ref[:, base:base + HEAD_DIM] = tb[r * lt:(r + 1) * lt]


def _qkv(x, w_qkv, tables, group, dil, seq, heads):
    m, d = x.shape
    aw = heads * HEAD_DIM
    tm = _row_tile(seq, 1024)
    spt = seq // tm
    nc = min(aw, 2 * HEAD_DIM)
    kern = functools.partial(_qkv_kernel, heads=heads, dil=dil, nc=nc)
    tab_spec = pl.BlockSpec((tm, HEAD_DIM), lambda i: (i % spt, 0))
    tabs = [_tile_deinterleave(t, tm, dil) for t in tables]
    return pl.pallas_call(
        kern,
        grid=(m // tm,),
        in_specs=[
            pl.BlockSpec((tm, d), lambda i: (i, 0)),
            pl.BlockSpec((d, N_QKV * aw), lambda i: (0, group), pipeline_mode=pl.Buffered(1)),
            tab_spec, tab_spec, tab_spec,
        ],
        out_specs=pl.BlockSpec((tm // dil, N_QKV * dil * aw), lambda i: (i, 0)),
        out_shape=jax.ShapeDtypeStruct((m // dil, N_QKV * dil * aw), BF16),
        scratch_shapes=[pltpu.VMEM((tm, d), BF16)]
        + ([pltpu.VMEM((tm, LANES), F32)] * 2 if dil > 1 else []),
        compiler_params=_params("parallel"),
        name=f"qkv_rope_g{group}",
    )(x, w_qkv, *tabs)


def _band_attn_kernel(q_ref, kp_ref, kc_ref, kn_ref, vp_ref, vc_ref, vn_ref,
                      o_ref, lse_ref, kcat_ref, vcat_ref, *, heads, sub_len, half):
    tq = q_ref.shape[0]
    nk = tq + 2 * half
    blk = pl.program_id(2)
    kcat_ref[0:half] = kp_ref[...]
    kcat_ref[half:half + tq] = kc_ref[...]
    kcat_ref[half + tq:nk] = kn_ref[...]
    vcat_ref[0:half] = vp_ref[...]
    vcat_ref[half:half + tq] = vc_ref[...]
    vcat_ref[half + tq:nk] = vn_ref[...]

    sq = min(tq, 2 * half)
    wk = sq + 2 * half
    row = lax.broadcasted_iota(jnp.int32, (sq, wk), 0)
    col = lax.broadcasted_iota(jnp.int32, (sq, wk), 1)
    rel = col - row
    band = (rel >= 0) & (rel <= 2 * half)
    lane = lax.broadcasted_iota(jnp.int32, (sq, LANES), 1)
    for j in range(tq // sq):
        key = blk * tq + j * sq - half + col
        mask = band & (key >= 0) & (key < sub_len)
        rows = slice(j * sq, (j + 1) * sq)
        keys = slice(j * sq, j * sq + wk)
        lse_all = jnp.zeros((sq, LANES), F32)
        for hd in range(heads):
            cols = slice(hd * HEAD_DIM, (hd + 1) * HEAD_DIM)
            s = lax.dot_general(q_ref[rows, cols], kcat_ref[keys, cols],
                                (((1,), (1,)), ((), ())), preferred_element_type=F32)
            s = jnp.where(mask, s, NEG_BIG)
            mx = jnp.max(s, axis=-1, keepdims=True)
            p = jnp.exp2(s - mx)
            den = jnp.sum(p, axis=-1, keepdims=True)
            o = _mm(p.astype(BF16), vcat_ref[keys, cols])
            o_ref[rows, cols] = o / den
            lse_all = jnp.where(lane == hd, mx * math.log(2.0) + jnp.log(den), lse_all)
        lse_ref[rows, :] = lse_all


def _band_attn(qkv, group, dil, half, batch, seq, heads):
    aw = heads * HEAD_DIM
    sub_len = seq // dil
    tq = _row_tile(sub_len, 4 * half)
    hb = tq // half
    n_halo = sub_len // half
    view = qkv.reshape(batch, sub_len, N_QKV * dil * aw)

    def q_map(b, r, l):
        return (b, l, r)

    def cur_map(kind):
        return lambda b, r, l: (b, l, kind * dil + r)

    def prev_map(kind):
        return lambda b, r, l: (b, jnp.maximum(l * hb - 1, 0), kind * dil + r)

    def next_map(kind):
        return lambda b, r, l: (b, jnp.minimum((l + 1) * hb, n_halo - 1), kind * dil + r)

    cur = lambda kind: pl.BlockSpec((None, tq, aw), cur_map(kind))
    prev = lambda kind: pl.BlockSpec((None, half, aw), prev_map(kind))
    nxt = lambda kind: pl.BlockSpec((None, half, aw), next_map(kind))

    kern = functools.partial(_band_attn_kernel, heads=heads, sub_len=sub_len, half=half)
    o, lse = pl.pallas_call(
        kern,
        grid=(batch, dil, sub_len // tq),
        in_specs=[pl.BlockSpec((None, tq, aw), q_map),
                  prev(1), cur(1), nxt(1), prev(2), cur(2), nxt(2)],
        out_specs=[
            pl.BlockSpec((None, tq, aw), lambda b, r, l: (b, l, r)),
            pl.BlockSpec((None, tq, LANES), lambda b, r, l: (b, l, r)),
        ],
        out_shape=[
            jax.ShapeDtypeStruct((batch, sub_len, dil * aw), F32),
            jax.ShapeDtypeStruct((batch, sub_len, dil * LANES), F32),
        ],
        scratch_shapes=[pltpu.VMEM((tq + 2 * half, aw), BF16),
                        pltpu.VMEM((tq + 2 * half, aw), BF16)],
        compiler_params=_params("parallel", "parallel", "parallel"),
        name=f"band_attn_g{group}",
    )(view, view, view, view, view, view, view)
    return (o.reshape(batch * sub_len, dil * aw), lse.reshape(batch * sub_len, dil * LANES))


def _attn_out_kernel(*refs, dils, heads, alpha):
    n_groups = len(dils)
    o_refs = list(refs[:n_groups])
    l_refs = list(refs[n_groups:2 * n_groups])
    x_ref, wo_ref, g_ref, b_ref, out_ref, a_ref = refs[2 * n_groups:2 * n_groups + 6]
    stage = refs[2 * n_groups + 6:]
    tm, d = out_ref.shape
    aw = heads * HEAD_DIM
    head_o = [[None] * heads for _ in dils]
    s = 0
    for g, dil in enumerate(dils):
        if dil == 1:
            for hd in range(heads):
                head_o[g][hd] = o_refs[g].at[:, hd * HEAD_DIM:(hd + 1) * HEAD_DIM]
            continue
        on_ref, ln_ref = stage[s], stage[s + 1]
        s += 2
        lt = tm // dil
        for r in range(dil):
            ln_ref[pl.ds(r, lt, stride=dil), :] = l_refs[g][:, r * LANES:(r + 1) * LANES]
            for hd in range(heads):
                col = r * aw + hd * HEAD_DIM
                on_ref[pl.ds(hd * tm + r, lt, stride=dil), :] = o_refs[g][:, col:col + HEAD_DIM]
        for hd in range(heads):
            head_o[g][hd] = on_ref.at[hd * tm:(hd + 1) * tm, :]
        l_refs[g] = ln_ref
    for r0 in range(0, tm, ROW_CHUNK):
        rows = slice(r0, r0 + ROW_CHUNK)
        ls = [r[rows, :] for r in l_refs]
        mx = functools.reduce(jnp.maximum, ls)
        es = [jnp.exp(l - mx) for l in ls]
        inv = 1.0 / functools.reduce(jnp.add, es)
        ws = [e * inv for e in es]
        for hd in range(heads):
            cols = slice(hd * HEAD_DIM, (hd + 1) * HEAD_DIM)
            acc = ws[0][:, hd:hd + 1] * head_o[0][hd][rows, :]
            for g in range(1, n_groups):
                acc = acc + ws[g][:, hd:hd + 1] * head_o[g][hd][rows, :]
            a_ref[rows, cols] = acc.astype(BF16)
        y = alpha * x_ref[rows, :] + _mm(a_ref[rows, :], wo_ref[...])
        out_ref[rows, :] = _layer_norm(y, g_ref[...], b_ref[...])


def _attn_out(os_, lses, dils, x, w_o, ln_g, ln_b, alpha, heads):
    m, d = x.shape
    aw = heads * HEAD_DIM
    tm = _row_tile(m, 512)
    kern = functools.partial(_attn_out_kernel, dils=tuple(dils), heads=heads, alpha=alpha)
    row = lambda w: pl.BlockSpec((tm, w), lambda i: (i, 0))
    packed = lambda w: [pl.BlockSpec((tm // dil, dil * w), lambda i: (i, 0)) for dil in dils]
    stage = []
    for dil in dils:
        if dil > 1:
            stage += [pltpu.VMEM((heads * tm, LANES), F32), pltpu.VMEM((tm, LANES), F32)]
    return pl.pallas_call(
        kern,
        grid=(m // tm,),
        in_specs=packed(aw) + packed(LANES) + [
            row(d), _resident(w_o.shape), _resident(ln_g.shape), _resident(ln_b.shape)],
        out_specs=row(d),
        out_shape=jax.ShapeDtypeStruct((m, d), F32),
        scratch_shapes=[pltpu.VMEM((tm, aw), BF16)] + stage,
        compiler_params=_params("parallel"),
        name="attn_out",
    )(*os_, *lses, x, w_o, ln_g, ln_b)


def _trunk(x, p, w):
    batch, seq, d = x.shape
    depth = p.shape[0]
    alpha = (2.0 * depth) ** 0.25
    m = batch * seq
    heads = w["w_o_b"].shape[1] // HEAD_DIM
    tables = _rope_tables(seq)
    h = x.reshape(m, d)
    for i in range(depth):
        j = i // 2
        ln_g = w["ln_mix_g"][i][None]
        ln_b = w["ln_mix_b"][i][None]
        if i % 2 == 0:
            u, v = _gmlp_in(h, w["w_in_a"][j], w["ln_v_g"][j][None], w["ln_v_b"][j][None])
            h = _gmlp_out(u, v, h, w["w_s_a"][j], w["sgu_bias"][j], w["w_o_a"][j], ln_g, ln_b, alpha)
        else:
            os_, lses = [], []
            for g, (window, dil) in enumerate(DILATION_PAIRS):
                qkv = _qkv(h, w["w_qkv_b"][j], tables, g, dil, seq, heads)
                o, lse = _band_attn(qkv, g, dil, window // (2 * dil), batch, seq, heads)
                os_.append(o)
                lses.append(lse)
            dils = [dil for _, dil in DILATION_PAIRS]
            h = _attn_out(os_, lses, dils, h, w["w_o_b"][j], ln_g, ln_b, alpha, heads)
        h = _ffn(h, w["w_ffn_gu"][i], w["w_ffn_down"][i],
                 w["ln_ffn_g"][i][None], w["ln_ffn_b"][i][None], alpha)
        h = _ple(h, p.reshape(depth, m, -1), i, w["w_ple_gate"][i], w["w_ple_proj"][i])
    return h.reshape(batch, seq, d)


def kernel(x_prompt, x_sample, p_prompt, p_sample, w_in_a, ln_v_g, ln_v_b, w_s_a, b_s_a, w_o_a, w_qkv_b, w_o_b, ln_mix_g, ln_mix_b, w_ffn_gu, w_ffn_down, ln_ffn_g, ln_ffn_b, w_ple_gate, w_ple_proj):
    group_dim = w_in_a.shape[2] // 2 // w_s_a.shape[1]
    assert group_dim == GROUP_DIM and w_s_a.shape[2] == CHUNK
    sgu_bias = jnp.repeat(jnp.swapaxes(b_s_a, 1, 2), group_dim, axis=2)
    aw = w_o_b.shape[1]
    is_q = (jnp.arange(w_qkv_b.shape[2]) // aw) % N_QKV == 0
    q_scale = jnp.where(is_q, HEAD_DIM ** -0.5 * math.log2(math.e), 1.0).astype(F32)
    w_qkv_b = w_qkv_b * q_scale
    w = dict(
        w_in_a=w_in_a.astype(BF16), ln_v_g=ln_v_g, ln_v_b=ln_v_b,
        w_s_a=w_s_a.astype(BF16), sgu_bias=sgu_bias, w_o_a=w_o_a.astype(BF16),
        w_qkv_b=w_qkv_b.astype(BF16), w_o_b=w_o_b.astype(BF16),
        ln_mix_g=ln_mix_g, ln_mix_b=ln_mix_b,
        w_ffn_gu=w_ffn_gu.astype(BF16), w_ffn_down=w_ffn_down.astype(BF16),
        ln_ffn_g=ln_ffn_g, ln_ffn_b=ln_ffn_b,
        w_ple_gate=w_ple_gate.astype(BF16), w_ple_proj=w_ple_proj.astype(BF16),
    )
    return (_trunk(x_prompt, p_prompt, w), _trunk(x_sample, p_sample, w))
```

```python
import functools
import math

import jax
import jax.numpy as jnp
from jax import lax
from jax.experimental import pallas as pl
from jax.experimental.pallas import tpu as pltpu

F32 = jnp.float32
BF16 = jnp.bfloat16

LN_EPS = 1e-5
ROPE_THETA = 500000.0
NEG_BIG = -1e30
HEAD_DIM = 128
ROT_DIM = HEAD_DIM // 4
CHUNK = 128
GROUP_DIM = 128
DILATION_PAIRS = ((128, 1), (512, 4), (2048, 16))
N_QKV = 3

V7X_VMEM_LIMIT_BYTES = 56 * 1024 * 1024
V7X_FFN_VMEM_LIMIT_BYTES = 58 * 1024 * 1024
LANES = 128
ROW_CHUNK = 128
ATTN_SUBBLOCKS_PER_STEP = 4
MAX_ROW_STRIDE = 8


def _params(*semantics, vmem_limit=V7X_VMEM_LIMIT_BYTES):
    return pltpu.CompilerParams(dimension_semantics=semantics, vmem_limit_bytes=vmem_limit)


def _resident(shape):
    nd = len(shape)
    return pl.BlockSpec(shape, lambda *_: (0,) * nd, pipeline_mode=pl.Buffered(1))


def _row_tile(m, pref):
    t = min(m, pref)
    assert m % t == 0, (m, t)
    return t


def _col_chunk(n, pref=512):
    c = min(n, pref)
    assert n % c == 0, (n, c)
    return c


def _layer_norm(y, g, b):
    mu = jnp.mean(y, axis=-1, keepdims=True)
    yc = y - mu
    var = jnp.mean(yc * yc, axis=-1, keepdims=True)
    return yc * lax.rsqrt(var + LN_EPS) * g + b


def _gelu(h):
    return 0.5 * h * (1.0 + lax.erf(h * math.sqrt(0.5)))


def _sigmoid(z):
    return 1.0 / (1.0 + jnp.exp(-z))


def _mm(a, b):
    return jnp.dot(a, b, preferred_element_type=F32)


def _gmlp_in_kernel(x_ref, w_ref, g_ref, b_ref, u_ref, v_ref, xb_ref, *, width, nc):
    tm = x_ref.shape[0]
    xb_ref[...] = x_ref[...].astype(BF16)
    for r in range(0, tm, ROW_CHUNK):
        rows = slice(r, r + ROW_CHUNK)
        hv = _gelu(_mm(xb_ref[rows, :], w_ref[:, width:]))
        v_ref[rows, :] = _layer_norm(hv, g_ref[...], b_ref[...]).astype(BF16)
    for n in range(0, width, nc):
        u_ref[:, n:n + nc] = _gelu(_mm(xb_ref[...], w_ref[:, n:n + nc]))


def _gmlp_in(x, w_in, ln_g, ln_b):
    m, d = x.shape
    width = w_in.shape[1] // 2
    tm = _row_tile(m, 512)
    kern = functools.partial(_gmlp_in_kernel, width=width, nc=_col_chunk(width))
    return pl.pallas_call(
        kern,
        grid=(m // tm,),
        in_specs=[
            pl.BlockSpec((tm, d), lambda i: (i, 0)),
            _resident(w_in.shape),
            _resident(ln_g.shape),
            _resident(ln_b.shape),
        ],
        out_specs=[
            pl.BlockSpec((tm, width), lambda i: (i, 0)),
            pl.BlockSpec((tm, width), lambda i: (i, 0)),
        ],
        out_shape=[
            jax.ShapeDtypeStruct((m, width), F32),
            jax.ShapeDtypeStruct((m, width), BF16),
        ],
        scratch_shapes=[pltpu.VMEM((tm, d), BF16)],
        compiler_params=_params("parallel"),
        name="gmlp_in",
    )(x, w_in, ln_g, ln_b)


def _gmlp_out_kernel(u_ref, v_ref, x_ref, ws_ref, bias_ref, wo_ref, g_ref, b_ref,
                     o_ref, t_ref, *, groups, alpha):
    tm, d = o_ref.shape
    for c in range(tm // CHUNK):
        rows = slice(c * CHUNK, (c + 1) * CHUNK)
        for g in range(groups):
            cols = slice(g * GROUP_DIM, (g + 1) * GROUP_DIM)
            sv = _mm(ws_ref[g], v_ref[rows, cols]) + bias_ref[:, cols]
            t_ref[rows, cols] = (u_ref[rows, cols] * sv).astype(BF16)
        y = alpha * x_ref[rows, :] + _mm(t_ref[rows, :], wo_ref[...])
        o_ref[rows, :] = _layer_norm(y, g_ref[...], b_ref[...])


def _gmlp_out(u, v, x, w_s, bias, w_o, ln_g, ln_b, alpha):
    m, d = x.shape
    width = u.shape[1]
    groups = w_s.shape[0]
    tm = _row_tile(m, 512)
    kern = functools.partial(_gmlp_out_kernel, groups=groups, alpha=alpha)
    return pl.pallas_call(
        kern,
        grid=(m // tm,),
        in_specs=[
            pl.BlockSpec((tm, width), lambda i: (i, 0)),
            pl.BlockSpec((tm, width), lambda i: (i, 0)),
            pl.BlockSpec((tm, d), lambda i: (i, 0)),
            _resident(w_s.shape),
            _resident(bias.shape),
            _resident(w_o.shape),
            _resident(ln_g.shape),
            _resident(ln_b.shape),
        ],
        out_specs=pl.BlockSpec((tm, d), lambda i: (i, 0)),
        out_shape=jax.ShapeDtypeStruct((m, d), F32),
        scratch_shapes=[pltpu.VMEM((tm, width), BF16)],
        compiler_params=_params("parallel"),
        name="gmlp_out",
    )(u, v, x, w_s, bias, w_o, ln_g, ln_b)


def _ffn_kernel(x_ref, wg_ref, wu_ref, wd_ref, g_ref, b_ref, o_ref, xb_ref, act_ref, *, alpha, nc):
    f = pl.program_id(1)

    @pl.when(f == 0)
    def _():
        xb_ref[...] = x_ref[...].astype(BF16)
        o_ref[...] = jnp.zeros_like(o_ref)

    hc = min(wg_ref.shape[1], 2 * LANES)
    for h0 in range(0, wg_ref.shape[1], hc):
        gate = _mm(xb_ref[...], wg_ref[:, h0:h0 + hc])
        up = _mm(xb_ref[...], wu_ref[:, h0:h0 + hc])
        act_ref[:, h0:h0 + hc] = (gate * _sigmoid(gate) * up).astype(BF16)
    for n in range(0, o_ref.shape[1], nc):
        o_ref[:, n:n + nc] += _mm(act_ref[...], wd_ref[:, n:n + nc])

    @pl.when(f == pl.num_programs(1) - 1)
    def _():
        for r in range(0, o_ref.shape[0], ROW_CHUNK):
            rows = slice(r, r + ROW_CHUNK)
            y = alpha * x_ref[rows, :] + o_ref[rows, :]
            o_ref[rows, :] = _layer_norm(y, g_ref[...], b_ref[...])


def _ffn(x, w_gu, w_down, ln_g, ln_b, alpha):
    m, d = x.shape
    dff = w_down.shape[0]
    tm = _row_tile(m, 1024)
    tf = _col_chunk(dff, 512)
    nf = dff // tf
    kern = functools.partial(_ffn_kernel, alpha=alpha, nc=_col_chunk(d))
    return pl.pallas_call(
        kern,
        grid=(m // tm, nf),
        in_specs=[
            pl.BlockSpec((tm, d), lambda i, f: (i, 0)),
            pl.BlockSpec((d, tf), lambda i, f: (0, f)),
            pl.BlockSpec((d, tf), lambda i, f: (0, nf + f)),
            pl.BlockSpec((tf, d), lambda i, f: (f, 0)),
            _resident(ln_g.shape),
            _resident(ln_b.shape),
        ],
        out_specs=pl.BlockSpec((tm, d), lambda i, f: (i, 0)),
        out_shape=jax.ShapeDtypeStruct((m, d), F32),
        scratch_shapes=[pltpu.VMEM((tm, d), BF16), pltpu.VMEM((tm, tf), BF16)],
        compiler_params=_params("parallel", "arbitrary", vmem_limit=V7X_FFN_VMEM_LIMIT_BYTES),
        name="ffn",
    )(x, w_gu, w_gu, w_down, ln_g, ln_b)


def _ple_kernel(x_ref, p_ref, wg_ref, wp_ref, o_ref, *, nc):
    d = o_ref.shape[1]
    xb = x_ref[...].astype(BF16)
    pb = p_ref[...].astype(BF16)
    for n in range(0, d, nc):
        gate = _sigmoid(_mm(xb, wg_ref[:, n:n + nc]))
        o_ref[:, n:n + nc] = x_ref[:, n:n + nc] + gate * _mm(pb, wp_ref[:, n:n + nc])


def _ple(x, p, layer, w_gate, w_proj):
    m, d = x.shape
    pd = p.shape[2]
    tm = _row_tile(m, 512)
    kern = functools.partial(_ple_kernel, nc=_col_chunk(d))
    return pl.pallas_call(
        kern,
        grid=(m // tm,),
        in_specs=[
            pl.BlockSpec((tm, d), lambda i: (i, 0)),
            pl.BlockSpec((None, tm, pd), lambda i: (layer, i, 0)),
            _resident(w_gate.shape),
            _resident(w_proj.shape),
        ],
        out_specs=pl.BlockSpec((tm, d), lambda i: (i, 0)),
        out_shape=jax.ShapeDtypeStruct((m, d), F32),
        compiler_params=_params("parallel"),
        name="ple",
    )(x, p, w_gate, w_proj)


def _rope_tables(seq):
    half = ROT_DIM // 2
    inv_freq = ROPE_THETA ** (-jnp.arange(0, ROT_DIM, 2, dtype=F32) / ROT_DIM)
    ang = jnp.arange(seq, dtype=F32)[:, None] * inv_freq[None, :]
    cos, sin = jnp.cos(ang), jnp.sin(ang)
    rest = HEAD_DIM - ROT_DIM
    c = jnp.concatenate([cos, cos, jnp.ones((seq, rest), F32)], axis=1)
    sa = jnp.concatenate([-sin, jnp.zeros((seq, HEAD_DIM - half), F32)], axis=1)
    sb = jnp.concatenate([jnp.zeros((seq, half), F32), sin, jnp.zeros((seq, rest), F32)], axis=1)
    return c, sa, sb


def _tile_deinterleave(tab, tile, dil):
    if dil == 1:
        return tab
    seq, w = tab.shape
    return tab.reshape(seq // tile, tile // dil, dil, w).swapaxes(1, 2).reshape(seq, w)


def _qkv_kernel(x_ref, w_ref, c_ref, sa_ref, sb_ref, o_ref, xb_ref, *stage, heads, dil, nc):
    tm = x_ref.shape[0]
    lt = tm // dil
    aw = heads * HEAD_DIM

    if dil == 1:
        xb_ref[...] = x_ref[...].astype(BF16)
    else:
        s1 = dil if dil <= MAX_ROW_STRIDE else MAX_ROW_STRIDE // 2
        s2 = dil // s1
        assert s1 * s2 == dil and s2 <= MAX_ROW_STRIDE
        xs_ref, xt_ref = stage
        for c in range(x_ref.shape[1] // LANES):
            cols = slice(c * LANES, (c + 1) * LANES)
            xs_ref[...] = x_ref[:, cols]
            if s2 == 1:
                for r in range(dil):
                    xb_ref[r * lt:(r + 1) * lt, cols] = (
                        xs_ref[pl.ds(r, lt, stride=dil), :].astype(BF16))
                continue
            l1 = tm // s1
            for r1 in range(s1):
                xt_ref[r1 * l1:(r1 + 1) * l1, :] = xs_ref[pl.ds(r1, l1, stride=s1), :]
            for r1 in range(s1):
                for r2 in range(s2):
                    r = r2 * s1 + r1
                    xb_ref[r * lt:(r + 1) * lt, cols] = (
                        xt_ref[pl.ds(r1 * l1 + r2, lt, stride=s2), :].astype(BF16))

    half = ROT_DIM // 2
    for kind in range(N_QKV):
        for n in range(0, aw, nc):
            h = _mm(xb_ref[...], w_ref[:, kind * aw + n:kind * aw + n + nc])
            for hd in range(nc // HEAD_DIM):
                t = h[:, hd * HEAD_DIM:(hd + 1) * HEAD_DIM]
                if kind < 2:
                    t = (t * c_ref[...] + pltpu.roll(t, HEAD_DIM - half, 1) * sa_ref[...]
                         + pltpu.roll(t, half, 1) * sb_ref[...])
                tb = t.astype(BF16)
                col = n + hd * HEAD_DIM
                for r in range(dil):
                    base = (kind * dil + r) * aw + col
                    o_ref[:, base:base + HEAD_DIM] = tb[r * lt:(r + 1) * lt]


def _qkv(x, w_qkv, tables, group, dil, seq, heads):
    m, d = x.shape
    aw = heads * HEAD_DIM
    tm = _row_tile(seq, 1024)
    spt = seq // tm
    nc = min(aw, 2 * HEAD_DIM)
    kern = functools.partial(_qkv_kernel, heads=heads, dil=dil, nc=nc)
    tab_spec = pl.BlockSpec((tm, HEAD_DIM), lambda i: (i % spt, 0))
    tabs = [_tile_deinterleave(t, tm, dil) for t in tables]
    return pl.pallas_call(
        kern,
        grid=(m // tm,),
        in_specs=[
            pl.BlockSpec((tm, d), lambda i: (i, 0)),
            pl.BlockSpec((d, N_QKV * aw), lambda i: (0, group), pipeline_mode=pl.Buffered(1)),
            tab_spec, tab_spec, tab_spec,
        ],
        out_specs=pl.BlockSpec((tm // dil, N_QKV * dil * aw), lambda i: (i, 0)),
        out_shape=jax.ShapeDtypeStruct((m // dil, N_QKV * dil * aw), BF16),
        scratch_shapes=[pltpu.VMEM((tm, d), BF16)]
        + ([pltpu.VMEM((tm, LANES), F32)] * 2 if dil > 1 else []),
        compiler_params=_params("parallel"),
        name=f"qkv_rope_g{group}",
    )(x, w_qkv, *tabs)


def _band_attn_kernel(q_ref, kp_ref, kc_ref, kn_ref, vp_ref, vc_ref, vn_ref,
                      o_ref, lse_ref, kcat_ref, vcat_ref, *, heads, sub_len, half):
    tq = q_ref.shape[0]
    nk = tq + 2 * half
    blk = pl.program_id(2)
    kcat_ref[0:half] = kp_ref[...]
    kcat_ref[half:half + tq] = kc_ref[...]
    kcat_ref[half + tq:nk] = kn_ref[...]
    vcat_ref[0:half] = vp_ref[...]
    vcat_ref[half:half + tq] = vc_ref[...]
    vcat_ref[half + tq:nk] = vn_ref[...]

    sq = min(tq, 2 * half)
    wk = sq + 2 * half
    row = lax.broadcasted_iota(jnp.int32, (sq, wk), 0)
    col = lax.broadcasted_iota(jnp.int32, (sq, wk), 1)
    rel = col - row
    band = (rel >= 0) & (rel <= 2 * half)
    lane = lax.broadcasted_iota(jnp.int32, (sq, LANES), 1)
    units = q_ref.shape[1] // (heads * HEAD_DIM)
    for j in range(tq // sq):
        key = blk * tq + j * sq - half + col
        mask = band & (key >= 0) & (key < sub_len)
        rows = slice(j * sq, (j + 1) * sq)
        keys = slice(j * sq, j * sq + wk)
        for u in range(units):
            mx_all = jnp.zeros((sq, LANES), F32)
            den_all = jnp.ones((sq, LANES), F32)
            for hd in range(heads):
                c0 = (u * heads + hd) * HEAD_DIM
                cols = slice(c0, c0 + HEAD_DIM)
                s = lax.dot_general(q_ref[rows, cols], kcat_ref[keys, cols],
                                    (((1,), (1,)), ((), ())), preferred_element_type=F32)
                s = jnp.where(mask, s, NEG_BIG)
                mx = jnp.max(s, axis=-1, keepdims=True)
                p = jnp.exp2(s - mx)
                den = jnp.sum(p, axis=-1, keepdims=True)
                o = _mm(p.astype(BF16), vcat_ref[keys, cols])
                o_ref[rows, cols] = o / den
                mx_all = jnp.where(lane == hd, mx, mx_all)
                den_all = jnp.where(lane == hd, den, den_all)
            lse_ref[rows, u * LANES:(u + 1) * LANES] = mx_all * math.log(2.0) + jnp.log(den_all)


def _band_attn(qkv, group, dil, half, batch, seq, heads):
    aw = heads * HEAD_DIM
    sub_len = seq // dil
    per_step = ATTN_SUBBLOCKS_PER_STEP * 2 * half
    tq = _row_tile(sub_len, per_step)
    hb = tq // half
    n_halo = sub_len // half
    units = max(1, min(dil, per_step // tq))
    assert dil % units == 0
    nres = dil // units
    uw = units * aw
    view = qkv.reshape(batch, sub_len, N_QKV * dil * aw)

    def q_map(b, r, l):
        return (b, l, r)

    def cur_map(kind):
        return lambda b, r, l: (b, l, kind * nres + r)

    def prev_map(kind):
        return lambda b, r, l: (b, jnp.maximum(l * hb - 1, 0), kind * nres + r)

    def next_map(kind):
        return lambda b, r, l: (b, jnp.minimum((l + 1) * hb, n_halo - 1), kind * nres + r)

    cur = lambda kind: pl.BlockSpec((None, tq, uw), cur_map(kind))
    prev = lambda kind: pl.BlockSpec((None, half, uw), prev_map(kind))
    nxt = lambda kind: pl.BlockSpec((None, half, uw), next_map(kind))

    kern = functools.partial(_band_attn_kernel, heads=heads, sub_len=sub_len, half=half)
    o, lse = pl.pallas_call(
        kern,
        grid=(batch, nres, sub_len // tq),
        in_specs=[pl.BlockSpec((None, tq, uw), q_map),
                  prev(1), cur(1), nxt(1), prev(2), cur(2), nxt(2)],
        out_specs=[
            pl.BlockSpec((None, tq, uw), lambda b, r, l: (b, l, r)),
            pl.BlockSpec((None, tq, units * LANES), lambda b, r, l: (b, l, r)),
        ],
        out_shape=[
            jax.ShapeDtypeStruct((batch, sub_len, dil * aw), F32),
            jax.ShapeDtypeStruct((batch, sub_len, dil * LANES), F32),
        ],
        scratch_shapes=[pltpu.VMEM((tq + 2 * half, uw), BF16),
                        pltpu.VMEM((tq + 2 * half, uw), BF16)],
        compiler_params=_params("parallel", "parallel", "parallel"),
        name=f"band_attn_g{group}",
    )(view, view, view, view, view, view, view)
    return (o.reshape(batch * sub_len, dil * aw), lse.reshape(batch * sub_len, dil * LANES))


def _attn_out_kernel(*refs, dils, heads, alpha):
    n_groups = len(dils)
    o_refs = list(refs[:n_groups])
    l_refs = list(refs[n_groups:2 * n_groups])
    x_ref, wo_ref, g_ref, b_ref, out_ref = refs[2 * n_groups:2 * n_groups + 5]
    stage = refs[2 * n_groups + 5:]
    tm, d = out_ref.shape
    aw = heads * HEAD_DIM
    head_o = [[None] * heads for _ in dils]
    s = 0
    for g, dil in enumerate(dils):
        if dil == 1:
            for hd in range(heads):
                head_o[g][hd] = o_refs[g].at[:, hd * HEAD_DIM:(hd + 1) * HEAD_DIM]
            continue
        on_ref, ln_ref = stage[s], stage[s + 1]
        s += 2
        lt = tm // dil
        for r in range(dil):
            ln_ref[pl.ds(r, lt, stride=dil), :] = l_refs[g][:, r * LANES:(r + 1) * LANES]
            for hd in range(heads):
                col = r * aw + hd * HEAD_DIM
                on_ref[pl.ds(hd * tm + r, lt, stride=dil), :] = o_refs[g][:, col:col + HEAD_DIM]
        for hd in range(heads):
            head_o[g][hd] = on_ref.at[hd * tm:(hd + 1) * tm, :]
        l_refs[g] = ln_ref
    for r0 in range(0, tm, ROW_CHUNK):
        rows = slice(r0, r0 + ROW_CHUNK)
        ls = [r[rows, :] for r in l_refs]
        mx = functools.reduce(jnp.maximum, ls)
        es = [jnp.exp(l - mx) for l in ls]
        inv = 1.0 / functools.reduce(jnp.add, es)
        ws = [e * inv for e in es]
        parts = []
        for hd in range(heads):
            acc = ws[0][:, hd:hd + 1] * head_o[0][hd][rows, :]
            for g in range(1, n_groups):
                acc = acc + ws[g][:, hd:hd + 1] * head_o[g][hd][rows, :]
            parts.append(acc.astype(BF16))
        a = jnp.concatenate(parts, axis=1)
        y = alpha * x_ref[rows, :] + _mm(a, wo_ref[...])
        out_ref[rows, :] = _layer_norm(y, g_ref[...], b_ref[...])


def _attn_out(os_, lses, dils, x, w_o, ln_g, ln_b, alpha, heads):
    m, d = x.shape
    aw = heads * HEAD_DIM
    tm = _row_tile(m, 512)
    kern = functools.partial(_attn_out_kernel, dils=tuple(dils), heads=heads, alpha=alpha)
    row = lambda w: pl.BlockSpec((tm, w), lambda i: (i, 0))
    packed = lambda w: [pl.BlockSpec((tm // dil, dil * w), lambda i: (i, 0)) for dil in dils]
    stage = []
    for dil in dils:
        if dil > 1:
            stage += [pltpu.VMEM((heads * tm, LANES), F32), pltpu.VMEM((tm, LANES), F32)]
    return pl.pallas_call(
        kern,
        grid=(m // tm,),
        in_specs=packed(aw) + packed(LANES) + [
            row(d), _resident(w_o.shape), _resident(ln_g.shape), _resident(ln_b.shape)],
        out_specs=row(d),
        out_shape=jax.ShapeDtypeStruct((m, d), F32),
        scratch_shapes=stage,
        compiler_params=_params("parallel"),
        name="attn_out",
    )(*os_, *lses, x, w_o, ln_g, ln_b)


def _trunk(x, p, w):
    batch, seq, d = x.shape
    depth = p.shape[0]
    alpha = (2.0 * depth) ** 0.25
    m = batch * seq
    heads = w["w_o_b"].shape[1] // HEAD_DIM
    tables = _rope_tables(seq)
    h = x.reshape(m, d)
    for i in range(depth):
        j = i // 2
        ln_g = w["ln_mix_g"][i][None]
        ln_b = w["ln_mix_b"][i][None]
        if i % 2 == 0:
            u, v = _gmlp_in(h, w["w_in_a"][j], w["ln_v_g"][j][None], w["ln_v_b"][j][None])
            h = _gmlp_out(u, v, h, w["w_s_a"][j], w["sgu_bias"][j], w["w_o_a"][j], ln_g, ln_b, alpha)
        else:
            os_, lses = [], []
            for g, (window, dil) in enumerate(DILATION_PAIRS):
                qkv = _qkv(h, w["w_qkv_b"][j], tables, g, dil, seq, heads)
                o, lse = _band_attn(qkv, g, dil, window // (2 * dil), batch, seq, heads)
                os_.append(o)
                lses.append(lse)
            dils = [dil for _, dil in DILATION_PAIRS]
            h = _attn_out(os_, lses, dils, h, w["w_o_b"][j], ln_g, ln_b, alpha, heads)
        h = _ffn(h, w["w_ffn_gu"][i], w["w_ffn_down"][i],
                 w["ln_ffn_g"][i][None], w["ln_ffn_b"][i][None], alpha)
        h = _ple(h, p.reshape(depth, m, -1), i, w["w_ple_gate"][i], w["w_ple_proj"][i])
    return h.reshape(batch, seq, d)


def kernel(x_prompt, x_sample, p_prompt, p_sample, w_in_a, ln_v_g, ln_v_b, w_s_a, b_s_a, w_o_a, w_qkv_b, w_o_b, ln_mix_g, ln_mix_b, w_ffn_gu, w_ffn_down, ln_ffn_g, ln_ffn_b, w_ple_gate, w_ple_proj):
    group_dim = w_in_a.shape[2] // 2 // w_s_a.shape[1]
    assert group_dim == GROUP_DIM and w_s_a.shape[2] == CHUNK
    sgu_bias = jnp.repeat(jnp.swapaxes(b_s_a, 1, 2), group_dim, axis=2)
    aw = w_o_b.shape[1]
    is_q = (jnp.arange(w_qkv_b.shape[2]) // aw) % N_QKV == 0
    q_scale = jnp.where(is_q, HEAD_DIM ** -0.5 * math.log2(math.e), 1.0).astype(F32)
    w_qkv_b = w_qkv_b * q_scale
    w = dict(
        w_in_a=w_in_a.astype(BF16), ln_v_g=ln_v_g, ln_v_b=ln_v_b,
        w_s_a=w_s_a.astype(BF16), sgu_bias=sgu_bias, w_o_a=w_o_a.astype(BF16),
        w_qkv_b=w_qkv_b.astype(BF16), w_o_b=w_o_b.astype(BF16),
        ln_mix_g=ln_mix_g, ln_mix_b=ln_mix_b,
        w_ffn_gu=w_ffn_gu.astype(BF16), w_ffn_down=w_ffn_down.astype(BF16),
        ln_ffn_g=ln_ffn_g, ln_ffn_b=ln_ffn_b,
        w_ple_gate=w_ple_gate.astype(BF16), w_ple_proj=w_ple_proj.astype(BF16),
    )
    return (_trunk(x_prompt, p_prompt, w), _trunk(x_sample, p_sample, w))
```

```python
import functools
import math

import jax
import jax.numpy as jnp
from jax import lax
from jax.experimental import pallas as pl
from jax.experimental.pallas import tpu as pltpu

F32 = jnp.float32
BF16 = jnp.bfloat16

LN_EPS = 1e-5
ROPE_THETA = 500000.0
NEG_BIG = -1e30
HEAD_DIM = 128
ROT_DIM = HEAD_DIM // 4
CHUNK = 128
GROUP_DIM = 128
DILATION_PAIRS = ((128, 1), (512, 4), (2048, 16))
N_QKV = 3

V7X_VMEM_LIMIT_BYTES = 56 * 1024 * 1024
V7X_FFN_VMEM_LIMIT_BYTES = 58 * 1024 * 1024
LANES = 128
ROW_CHUNK = 128
ATTN_SUBBLOCKS_PER_STEP = 8
MAX_ROW_STRIDE = 8


def _params(*semantics, vmem_limit=V7X_VMEM_LIMIT_BYTES):
    return pltpu.CompilerParams(dimension_semantics=semantics, vmem_limit_bytes=vmem_limit)


def _resident(shape):
    nd = len(shape)
    return pl.BlockSpec(shape, lambda *_: (0,) * nd, pipeline_mode=pl.Buffered(1))


def _resident_layer(shape, layer):
    nd = len(shape) - 1
    return pl.BlockSpec((None,) + tuple(shape[1:]), lambda *_: (layer,) + (0,) * nd,
                        pipeline_mode=pl.Buffered(1))


def _row_tile(m, pref):
    t = min(m, pref)
    assert m % t == 0, (m, t)
    return t


def _col_chunk(n, pref=512):
    c = min(n, pref)
    assert n % c == 0, (n, c)
    return c


def _layer_norm(y, g, b):
    mu = jnp.mean(y, axis=-1, keepdims=True)
    yc = y - mu
    var = jnp.mean(yc * yc, axis=-1, keepdims=True)
    return yc * lax.rsqrt(var + LN_EPS) * g + b


def _gelu(h):
    return 0.5 * h * (1.0 + lax.erf(h * math.sqrt(0.5)))


def _sigmoid(z):
    return 1.0 / (1.0 + jnp.exp(-z))


def _mm(a, b):
    return jnp.dot(a, b, preferred_element_type=F32)


def _gmlp_in_kernel(x_ref, w_ref, g_ref, b_ref, u_ref, v_ref, xb_ref, *, width, nc):
    tm = x_ref.shape[0]
    xb_ref[...] = x_ref[...].astype(BF16)
    for r in range(0, tm, ROW_CHUNK):
        rows = slice(r, r + ROW_CHUNK)
        hv = _gelu(_mm(xb_ref[rows, :], w_ref[:, width:]))
        v_ref[rows, :] = _layer_norm(hv, g_ref[...], b_ref[...]).astype(BF16)
    for n in range(0, width, nc):
        u_ref[:, n:n + nc] = _gelu(_mm(xb_ref[...], w_ref[:, n:n + nc]))


def _gmlp_in(x, w_in, ln_g, ln_b):
    m, d = x.shape
    width = w_in.shape[1] // 2
    tm = _row_tile(m, 512)
    kern = functools.partial(_gmlp_in_kernel, width=width, nc=_col_chunk(width))
    return pl.pallas_call(
        kern,
        grid=(m // tm,),
        in_specs=[
            pl.BlockSpec((tm, d), lambda i: (i, 0)),
            _resident(w_in.shape),
            _resident(ln_g.shape),
            _resident(ln_b.shape),
        ],
        out_specs=[
            pl.BlockSpec((tm, width), lambda i: (i, 0)),
            pl.BlockSpec((tm, width), lambda i: (i, 0)),
        ],
        out_shape=[
            jax.ShapeDtypeStruct((m, width), F32),
            jax.ShapeDtypeStruct((m, width), BF16),
        ],
        scratch_shapes=[pltpu.VMEM((tm, d), BF16)],
        compiler_params=_params("parallel"),
        name="gmlp_in",
    )(x, w_in, ln_g, ln_b)


def _gmlp_out_kernel(u_ref, v_ref, x_ref, ws_ref, bias_ref, wo_ref, g_ref, b_ref,
                     o_ref, t_ref, *, groups, alpha):
    tm, d = o_ref.shape
    for c in range(tm // CHUNK):
        rows = slice(c * CHUNK, (c + 1) * CHUNK)
        for g in range(groups):
            cols = slice(g * GROUP_DIM, (g + 1) * GROUP_DIM)
            sv = _mm(ws_ref[g], v_ref[rows, cols]) + bias_ref[:, cols]
            t_ref[rows, cols] = (u_ref[rows, cols] * sv).astype(BF16)
        y = alpha * x_ref[rows, :] + _mm(t_ref[rows, :], wo_ref[...])
        o_ref[rows, :] = _layer_norm(y, g_ref[...], b_ref[...])


def _gmlp_out(u, v, x, w_s, bias, w_o, ln_g, ln_b, alpha):
    m, d = x.shape
    width = u.shape[1]
    groups = w_s.shape[0]
    tm = _row_tile(m, 512)
    kern = functools.partial(_gmlp_out_kernel, groups=groups, alpha=alpha)
    return pl.pallas_call(
        kern,
        grid=(m // tm,),
        in_specs=[
            pl.BlockSpec((tm, width), lambda i: (i, 0)),
            pl.BlockSpec((tm, width), lambda i: (i, 0)),
            pl.BlockSpec((tm, d), lambda i: (i, 0)),
            _resident(w_s.shape),
            _resident(bias.shape),
            _resident(w_o.shape),
            _resident(ln_g.shape),
            _resident(ln_b.shape),
        ],
        out_specs=pl.BlockSpec((tm, d), lambda i: (i, 0)),
        out_shape=jax.ShapeDtypeStruct((m, d), F32),
        scratch_shapes=[pltpu.VMEM((tm, width), BF16)],
        compiler_params=_params("parallel"),
        name="gmlp_out",
    )(u, v, x, w_s, bias, w_o, ln_g, ln_b)


def _ffn_kernel(x_ref, wg_ref, wu_ref, wd_ref, g_ref, b_ref, o_ref, xb_ref, act_ref, *, alpha, nc):
    f = pl.program_id(1)

    @pl.when(f == 0)
    def _():
        xb_ref[...] = x_ref[...].astype(BF16)
        o_ref[...] = alpha * x_ref[...]

    hc = min(wg_ref.shape[1], 2 * LANES)
    for h0 in range(0, wg_ref.shape[1], hc):
        gate = _mm(xb_ref[...], wg_ref[:, h0:h0 + hc])
        up = _mm(xb_ref[...], wu_ref[:, h0:h0 + hc])
        act_ref[:, h0:h0 + hc] = (gate * _sigmoid(gate) * up).astype(BF16)
    for n in range(0, o_ref.shape[1], nc):
        o_ref[:, n:n + nc] += _mm(act_ref[...], wd_ref[:, n:n + nc])

    @pl.when(f == pl.num_programs(1) - 1)
    def _():
        for r in range(0, o_ref.shape[0], ROW_CHUNK):
            rows = slice(r, r + ROW_CHUNK)
            o_ref[rows, :] = _layer_norm(o_ref[rows, :], g_ref[...], b_ref[...])


def _ffn(x, w_gu, w_down, layer, ln_g, ln_b, alpha):
    m, d = x.shape
    dff = w_down.shape[1]
    tm = _row_tile(m, 1024)
    tf = _col_chunk(dff, 512)
    nf = dff // tf
    kern = functools.partial(_ffn_kernel, alpha=alpha, nc=_col_chunk(d))
    return pl.pallas_call(
        kern,
        grid=(m // tm, nf),
        in_specs=[
            pl.BlockSpec((tm, d), lambda i, f: (i, 0)),
            pl.BlockSpec((None, d, tf), lambda i, f: (layer, 0, f)),
            pl.BlockSpec((None, d, tf), lambda i, f: (layer, 0, nf + f)),
            pl.BlockSpec((None, tf, d), lambda i, f: (layer, f, 0)),
            _resident(ln_g.shape),
            _resident(ln_b.shape),
        ],
        out_specs=pl.BlockSpec((tm, d), lambda i, f: (i, 0)),
        out_shape=jax.ShapeDtypeStruct((m, d), F32),
        scratch_shapes=[pltpu.VMEM((tm, d), BF16), pltpu.VMEM((tm, tf), BF16)],
        compiler_params=_params("parallel", "arbitrary", vmem_limit=V7X_FFN_VMEM_LIMIT_BYTES),
        name="ffn",
    )(x, w_gu, w_gu, w_down, ln_g, ln_b)


def _ple_kernel(x_ref, p_ref, wg_ref, wp_ref, o_ref, *, nc):
    d = o_ref.shape[1]
    xb = x_ref[...].astype(BF16)
    pb = p_ref[...].astype(BF16)
    for n in range(0, d, nc):
        gate = _sigmoid(_mm(xb, wg_ref[:, n:n + nc]))
        o_ref[:, n:n + nc] = x_ref[:, n:n + nc] + gate * _mm(pb, wp_ref[:, n:n + nc])


def _ple(x, p, layer, w_gate, w_proj):
    m, d = x.shape
    pd = p.shape[2]
    tm = _row_tile(m, 512)
    kern = functools.partial(_ple_kernel, nc=_col_chunk(d))
    return pl.pallas_call(
        kern,
        grid=(m // tm,),
        in_specs=[
            pl.BlockSpec((tm, d), lambda i: (i, 0)),
            pl.BlockSpec((None, tm, pd), lambda i: (layer, i, 0)),
            _resident_layer(w_gate.shape, layer),
            _resident_layer(w_proj.shape, layer),
        ],
        out_specs=pl.BlockSpec((tm, d), lambda i: (i, 0)),
        out_shape=jax.ShapeDtypeStruct((m, d), F32),
        compiler_params=_params("parallel"),
        name="ple",
    )(x, p, w_gate, w_proj)


def _rope_tables(seq):
    half = ROT_DIM // 2
    inv_freq = ROPE_THETA ** (-jnp.arange(0, ROT_DIM, 2, dtype=F32) / ROT_DIM)
    ang = jnp.arange(seq, dtype=F32)[:, None] * inv_freq[None, :]
    cos, sin = jnp.cos(ang), jnp.sin(ang)
    rest = HEAD_DIM - ROT_DIM
    c = jnp.concatenate([cos, cos, jnp.ones((seq, rest), F32)], axis=1)
    sa = jnp.concatenate([-sin, jnp.zeros((seq, HEAD_DIM - half), F32)], axis=1)
    sb = jnp.concatenate([jnp.zeros((seq, half), F32), sin, jnp.zeros((seq, rest), F32)], axis=1)
    return c, sa, sb


def _tile_deinterleave(tab, tile, dil):
    if dil == 1:
        return tab
    seq, w = tab.shape
    return tab.reshape(seq // tile, tile // dil, dil, w).swapaxes(1, 2).reshape(seq, w)


def _qkv_kernel(x_ref, w_ref, c_ref, sa_ref, sb_ref, o_ref, xb_ref, *stage, heads, dil, nc):
    tm = x_ref.shape[0]
    lt = tm // dil
    aw = heads * HEAD_DIM

    if dil == 1:
        xb_ref[...] = x_ref[...].astype(BF16)
    else:
        s1 = dil if dil <= MAX_ROW_STRIDE else MAX_ROW_STRIDE // 2
        s2 = dil // s1
        assert s1 * s2 == dil and s2 <= MAX_ROW_STRIDE
        xs_ref, xt_ref = stage
        for c in range(x_ref.shape[1] // LANES):
            cols = slice(c * LANES, (c + 1) * LANES)
            xs_ref[...] = x_ref[:, cols]
            if s2 == 1:
                for r in range(dil):
                    xb_ref[r * lt:(r + 1) * lt, cols] = (
                        xs_ref[pl.ds(r, lt, stride=dil), :].astype(BF16))
                continue
            l1 = tm // s1
            for r1 in range(s1):
                xt_ref[r1 * l1:(r1 + 1) * l1, :] = xs_ref[pl.ds(r1, l1, stride=s1), :]
            for r1 in range(s1):
                for r2 in range(s2):
                    r = r2 * s1 + r1
                    xb_ref[r * lt:(r + 1) * lt, cols] = (
                        xt_ref[pl.ds(r1 * l1 + r2, lt, stride=s2), :].astype(BF16))

    half = ROT_DIM // 2
    for kind in range(N_QKV):
        for n in range(0, aw, nc):
            h = _mm(xb_ref[...], w_ref[:, kind * aw + n:kind * aw + n + nc])
            for hd in range(nc // HEAD_DIM):
                t = h[:, hd * HEAD_DIM:(hd + 1) * HEAD_DIM]
                if kind < 2:
                    t = (t * c_ref[...] + pltpu.roll(t, HEAD_DIM - half, 1) * sa_ref[...]
                         + pltpu.roll(t, half, 1) * sb_ref[...])
                tb = t.astype(BF16)
                col = n + hd * HEAD_DIM
                for r in range(dil):
                    base = (kind * dil + r) * aw + col
                    o_ref[:, base:base + HEAD_DIM] = tb[r * lt:(r + 1) * lt]


def _qkv(x, w_qkv, tables, group, dil, seq, heads):
    m, d = x.shape
    aw = heads * HEAD_DIM
    tm = _row_tile(seq, 1024)
    spt = seq // tm
    nc = min(aw, 2 * HEAD_DIM)
    kern = functools.partial(_qkv_kernel, heads=heads, dil=dil, nc=nc)
    tab_spec = pl.BlockSpec((tm, HEAD_DIM), lambda i: (i % spt, 0))
    tabs = [_tile_deinterleave(t, tm, dil) for t in tables]
    return pl.pallas_call(
        kern,
        grid=(m // tm,),
        in_specs=[
            pl.BlockSpec((tm, d), lambda i: (i, 0)),
            pl.BlockSpec((d, N_QKV * aw), lambda i: (0, group), pipeline_mode=pl.Buffered(1)),
            tab_spec, tab_spec, tab_spec,
        ],
        out_specs=pl.BlockSpec((tm // dil, N_QKV * dil * aw), lambda i: (i, 0)),
        out_shape=jax.ShapeDtypeStruct((m // dil, N_QKV * dil * aw), BF16),
        scratch_shapes=[pltpu.VMEM((tm, d), BF16)]
        + ([pltpu.VMEM((tm, LANES), F32)] * 2 if dil > 1 else []),
        compiler_params=_params("parallel"),
        name=f"qkv_rope_g{group}",
    )(x, w_qkv, *tabs)


def _band_attn_kernel(q_ref, kp_ref, kc_ref, kn_ref, vp_ref, vc_ref, vn_ref,
                      o_ref, lse_ref, kcat_ref, vcat_ref, *, heads, sub_len, half):
    tq = q_ref.shape[0]
    nk = tq + 2 * half
    blk = pl.program_id(2)
    kcat_ref[0:half] = kp_ref[...]
    kcat_ref[half:half + tq] = kc_ref[...]
    kcat_ref[half + tq:nk] = kn_ref[...]
    vcat_ref[0:half] = vp_ref[...]
    vcat_ref[half:half + tq] = vc_ref[...]
    vcat_ref[half + tq:nk] = vn_ref[...]

    sq = min(tq, 2 * half)
    wk = sq + 2 * half
    row = lax.broadcasted_iota(jnp.int32, (sq, wk), 0)
    col = lax.broadcasted_iota(jnp.int32, (sq, wk), 1)
    rel = col - row
    band = (rel >= 0) & (rel <= 2 * half)
    lane = lax.broadcasted_iota(jnp.int32, (sq, LANES), 1)
    units = q_ref.shape[1] // (heads * HEAD_DIM)
    for j in range(tq // sq):
        key = blk * tq + j * sq - half + col
        mask = band & (key >= 0) & (key < sub_len)
        rows = slice(j * sq, (j + 1) * sq)
        keys = slice(j * sq, j * sq + wk)
        for u in range(units):
            mx_all = jnp.zeros((sq, LANES), F32)
            den_all = jnp.ones((sq, LANES), F32)
            for hd in range(heads):
                c0 = (u * heads + hd) * HEAD_DIM
                cols = slice(c0, c0 + HEAD_DIM)
                s = lax.dot_general(q_ref[rows, cols], kcat_ref[keys, cols],
                                    (((1,), (1,)), ((), ())), preferred_element_type=F32)
                s = jnp.where(mask, s, NEG_BIG)
                mx = jnp.max(s, axis=-1, keepdims=True)
                p = jnp.exp2(s - mx)
                den = jnp.sum(p, axis=-1, keepdims=True)
                o = _mm(p.astype(BF16), vcat_ref[keys, cols])
                o_ref[rows, cols] = o / den
                mx_all = jnp.where(lane == hd, mx, mx_all)
                den_all = jnp.where(lane == hd, den, den_all)
            lse_ref[rows, u * LANES:(u + 1) * LANES] = mx_all * math.log(2.0) + jnp.log(den_all)


def _band_attn(qkv, group, dil, half, batch, seq, heads):
    aw = heads * HEAD_DIM
    sub_len = seq // dil
    per_step = ATTN_SUBBLOCKS_PER_STEP * 2 * half
    tq = _row_tile(sub_len, per_step)
    hb = tq // half
    n_halo = sub_len // half
    units = max(1, min(dil, per_step // tq))
    assert dil % units == 0
    nres = dil // units
    uw = units * aw
    view = qkv.reshape(batch, sub_len, N_QKV * dil * aw)

    def q_map(b, r, l):
        return (b, l, r)

    def cur_map(kind):
        return lambda b, r, l: (b, l, kind * nres + r)

    def prev_map(kind):
        return lambda b, r, l: (b, jnp.maximum(l * hb - 1, 0), kind * nres + r)

    def next_map(kind):
        return lambda b, r, l: (b, jnp.minimum((l + 1) * hb, n_halo - 1), kind * nres + r)

    cur = lambda kind: pl.BlockSpec((None, tq, uw), cur_map(kind))
    prev = lambda kind: pl.BlockSpec((None, half, uw), prev_map(kind))
    nxt = lambda kind: pl.BlockSpec((None, half, uw), next_map(kind))

    kern = functools.partial(_band_attn_kernel, heads=heads, sub_len=sub_len, half=half)
    o, lse = pl.pallas_call(
        kern,
        grid=(batch, nres, sub_len // tq),
        in_specs=[pl.BlockSpec((None, tq, uw), q_map),
                  prev(1), cur(1), nxt(1), prev(2), cur(2), nxt(2)],
        out_specs=[
            pl.BlockSpec((None, tq, uw), lambda b, r, l: (b, l, r)),
            pl.BlockSpec((None, tq, units * LANES), lambda b, r, l: (b, l, r)),
        ],
        out_shape=[
            jax.ShapeDtypeStruct((batch, sub_len, dil * aw), F32),
            jax.ShapeDtypeStruct((batch, sub_len, dil * LANES), F32),
        ],
        scratch_shapes=[pltpu.VMEM((tq + 2 * half, uw), BF16),
                        pltpu.VMEM((tq + 2 * half, uw), BF16)],
        compiler_params=_params("parallel", "parallel", "parallel"),
        name=f"band_attn_g{group}",
    )(view, view, view, view, view, view, view)
    return (o.reshape(batch * sub_len, dil * aw), lse.reshape(batch * sub_len, dil * LANES))


def _attn_out_kernel(*refs, dils, heads, alpha):
    n_groups = len(dils)
    o_refs = list(refs[:n_groups])
    l_refs = list(refs[n_groups:2 * n_groups])
    x_ref, wo_ref, g_ref, b_ref, out_ref = refs[2 * n_groups:2 * n_groups + 5]
    stage = refs[2 * n_groups + 5:]
    tm, d = out_ref.shape
    aw = heads * HEAD_DIM
    head_o = [[None] * heads for _ in dils]
    s = 0
    for g, dil in enumerate(dils):
        if dil == 1:
            for hd in range(heads):
                head_o[g][hd] = o_refs[g].at[:, hd * HEAD_DIM:(hd + 1) * HEAD_DIM]
            continue
        on_ref, ln_ref = stage[s], stage[s + 1]
        s += 2
        lt = tm // dil
        for r in range(dil):
            ln_ref[pl.ds(r, lt, stride=dil), :] = l_refs[g][:, r * LANES:(r + 1) * LANES]
            for hd in range(heads):
                col = r * aw + hd * HEAD_DIM
                on_ref[pl.ds(hd * tm + r, lt, stride=dil), :] = o_refs[g][:, col:col + HEAD_DIM]
        for hd in range(heads):
            head_o[g][hd] = on_ref.at[hd * tm:(hd + 1) * tm, :]
        l_refs[g] = ln_ref
    for r0 in range(0, tm, ROW_CHUNK):
        rows = slice(r0, r0 + ROW_CHUNK)
        ls = [r[rows, :] for r in l_refs]
        mx = functools.reduce(jnp.maximum, ls)
        es = [jnp.exp(l - mx) for l in ls]
        inv = 1.0 / functools.reduce(jnp.add, es)
        ws = [e * inv for e in es]
        parts = []
        for hd in range(heads):
            acc = ws[0][:, hd:hd + 1] * head_o[0][hd][rows, :]
            for g in range(1, n_groups):
                acc = acc + ws[g][:, hd:hd + 1] * head_o[g][hd][rows, :]
            parts.append(acc.astype(BF16))
        a = jnp.concatenate(parts, axis=1)
        y = alpha * x_ref[rows, :] + _mm(a, wo_ref[...])
        out_ref[rows, :] = _layer_norm(y, g_ref[...], b_ref[...])


def _attn_out(os_, lses, dils, x, w_o, ln_g, ln_b, alpha, heads):
    m, d = x.shape
    aw = heads * HEAD_DIM
    tm = _row_tile(m, 512)
    kern = functools.partial(_attn_out_kernel, dils=tuple(dils), heads=heads, alpha=alpha)
    row = lambda w: pl.BlockSpec((tm, w), lambda i: (i, 0))
    packed = lambda w: [pl.BlockSpec((tm // dil, dil * w), lambda i: (i, 0)) for dil in dils]
    stage = []
    for dil in dils:
        if dil > 1:
            stage += [pltpu.VMEM((heads * tm, LANES), F32), pltpu.VMEM((tm, LANES), F32)]
    return pl.pallas_call(
        kern,
        grid=(m // tm,),
        in_specs=packed(aw) + packed(LANES) + [
            row(d), _resident(w_o.shape), _resident(ln_g.shape), _resident(ln_b.shape)],
        out_specs=row(d),
        out_shape=jax.ShapeDtypeStruct((m, d), F32),
        scratch_shapes=stage,
        compiler_params=_params("parallel"),
        name="attn_out",
    )(*os_, *lses, x, w_o, ln_g, ln_b)


def _trunk(x, p, w):
    batch, seq, d = x.shape
    depth = p.shape[0]
    alpha = (2.0 * depth) ** 0.25
    m = batch * seq
    heads = w["w_o_b"].shape[1] // HEAD_DIM
    tables = _rope_tables(seq)
    h = x.reshape(m, d)
    for i in range(depth):
        j = i // 2
        ln_g = w["ln_mix_g"][i][None]
        ln_b = w["ln_mix_b"][i][None]
        if i % 2 == 0:
            u, v = _gmlp_in(h, w["w_in_a"][j], w["ln_v_g"][j][None], w["ln_v_b"][j][None])
            h = _gmlp_out(u, v, h, w["w_s_a"][j], w["sgu_bias"][j], w["w_o_a"][j], ln_g, ln_b, alpha)
        else:
            os_, lses = [], []
            for g, (window, dil) in enumerate(DILATION_PAIRS):
                qkv = _qkv(h, w["w_qkv_b"][j], tables, g, dil, seq, heads)
                o, lse = _band_attn(qkv, g, dil, window // (2 * dil), batch, seq, heads)
                os_.append(o)
                lses.append(lse)
            dils = [dil for _, dil in DILATION_PAIRS]
            h = _attn_out(os_, lses, dils, h, w["w_o_b"][j], ln_g, ln_b, alpha, heads)
        h = _ffn(h, w["w_ffn_gu"], w["w_ffn_down"], i,
                 w["ln_ffn_g"][i][None], w["ln_ffn_b"][i][None], alpha)
        h = _ple(h, p.reshape(depth, m, -1), i, w["w_ple_gate"], w["w_ple_proj"])
    return h.reshape(batch, seq, d)


def kernel(x_prompt, x_sample, p_prompt, p_sample, w_in_a, ln_v_g, ln_v_b, w_s_a, b_s_a, w_o_a, w_qkv_b, w_o_b, ln_mix_g, ln_mix_b, w_ffn_gu, w_ffn_down, ln_ffn_g, ln_ffn_b, w_ple_gate, w_ple_proj):
    group_dim = w_in_a.shape[2] // 2 // w_s_a.shape[1]
    assert group_dim == GROUP_DIM and w_s_a.shape[2] == CHUNK
    sgu_bias = jnp.repeat(jnp.swapaxes(b_s_a, 1, 2), group_dim, axis=2)
    aw = w_o_b.shape[1]
    is_q = (jnp.arange(w_qkv_b.shape[2]) // aw) % N_QKV == 0
    q_scale = jnp.where(is_q, HEAD_DIM ** -0.5 * math.log2(math.e), 1.0).astype(F32)
    w_qkv_b = w_qkv_b * q_scale
    w = dict(
        w_in_a=w_in_a.astype(BF16), ln_v_g=ln_v_g, ln_v_b=ln_v_b,
        w_s_a=w_s_a.astype(BF16), sgu_bias=sgu_bias, w_o_a=w_o_a.astype(BF16),
        w_qkv_b=w_qkv_b.astype(BF16), w_o_b=w_o_b.astype(BF16),
        ln_mix_g=ln_mix_g, ln_mix_b=ln_mix_b,
        w_ffn_gu=w_ffn_gu.astype(BF16), w_ffn_down=w_ffn_down.astype(BF16),
        ln_ffn_g=ln_ffn_g, ln_ffn_b=ln_ffn_b,
        w_ple_gate=w_ple_gate.astype(BF16), w_ple_proj=w_ple_proj.astype(BF16),
    )
    return (_trunk(x_prompt, p_prompt, w), _trunk(x_sample, p_sample, w))
```

```python
import functools
import math

import jax
import jax.numpy as jnp
from jax import lax
from jax.experimental import pallas as pl
from jax.experimental.pallas import tpu as pltpu

F32 = jnp.float32
BF16 = jnp.bfloat16

LN_EPS = 1e-5
ROPE_THETA = 500000.0
NEG_BIG = -1e30
HEAD_DIM = 128
ROT_DIM = HEAD_DIM // 4
CHUNK = 128
GROUP_DIM = 128
DILATION_PAIRS = ((128, 1), (512, 4), (2048, 16))
N_QKV = 3

V7X_VMEM_LIMIT_BYTES = 56 * 1024 * 1024
V7X_FFN_VMEM_LIMIT_BYTES = 58 * 1024 * 1024
LANES = 128
ROW_CHUNK = 128
ATTN_SUBBLOCKS_PER_STEP = 8
MAX_ROW_STRIDE = 8


def _params(*semantics, vmem_limit=V7X_VMEM_LIMIT_BYTES):
    return pltpu.CompilerParams(dimension_semantics=semantics, vmem_limit_bytes=vmem_limit)


def _resident(shape):
    nd = len(shape)
    return pl.BlockSpec(shape, lambda *_: (0,) * nd, pipeline_mode=pl.Buffered(1))


def _resident_layer(shape, layer):
    nd = len(shape) - 1
    return pl.BlockSpec((None,) + tuple(shape[1:]), lambda *_: (layer,) + (0,) * nd,
                        pipeline_mode=pl.Buffered(1))


def _row_tile(m, pref):
    t = min(m, pref)
    assert m % t == 0, (m, t)
    return t


def _col_chunk(n, pref=512):
    c = min(n, pref)
    assert n % c == 0, (n, c)
    return c


def _layer_norm(y, g, b):
    mu = jnp.mean(y, axis=-1, keepdims=True)
    yc = y - mu
    var = jnp.mean(yc * yc, axis=-1, keepdims=True)
    return yc * lax.rsqrt(var + LN_EPS) * g + b


def _gelu(h):
    return 0.5 * h * (1.0 + lax.erf(h * math.sqrt(0.5)))


def _sigmoid(z):
    return 1.0 / (1.0 + jnp.exp(-z))


def _mm(a, b):
    return jnp.dot(a, b, preferred_element_type=F32)


def _gmlp_in_kernel(x_ref, w_ref, g_ref, b_ref, u_ref, v_ref, xb_ref, *, width, nc):
    tm = x_ref.shape[0]
    xb_ref[...] = x_ref[...].astype(BF16)
    for r in range(0, tm, ROW_CHUNK):
        rows = slice(r, r + ROW_CHUNK)
        hv = _gelu(_mm(xb_ref[rows, :], w_ref[:, width:]))
        v_ref[rows, :] = _layer_norm(hv, g_ref[...], b_ref[...]).astype(BF16)
    for n in range(0, width, nc):
        u_ref[:, n:n + nc] = _gelu(_mm(xb_ref[...], w_ref[:, n:n + nc]))


def _gmlp_in(x, w_in, ln_g, ln_b):
    m, d = x.shape
    width = w_in.shape[1] // 2
    tm = _row_tile(m, 512)
    kern = functools.partial(_gmlp_in_kernel, width=width, nc=_col_chunk(width))
    return pl.pallas_call(
        kern,
        grid=(m // tm,),
        in_specs=[
            pl.BlockSpec((tm, d), lambda i: (i, 0)),
            _resident(w_in.shape),
            _resident(ln_g.shape),
            _resident(ln_b.shape),
        ],
        out_specs=[
            pl.BlockSpec((tm, width), lambda i: (i, 0)),
            pl.BlockSpec((tm, width), lambda i: (i, 0)),
        ],
        out_shape=[
            jax.ShapeDtypeStruct((m, width), F32),
            jax.ShapeDtypeStruct((m, width), BF16),
        ],
        scratch_shapes=[pltpu.VMEM((tm, d), BF16)],
        compiler_params=_params("parallel"),
        name="gmlp_in",
    )(x, w_in, ln_g, ln_b)


def _gmlp_out_kernel(u_ref, v_ref, x_ref, ws_ref, bias_ref, wo_ref, g_ref, b_ref,
                     o_ref, t_ref, *, groups, alpha):
    tm, d = o_ref.shape
    for c in range(tm // CHUNK):
        rows = slice(c * CHUNK, (c + 1) * CHUNK)
        for g in range(groups):
            cols = slice(g * GROUP_DIM, (g + 1) * GROUP_DIM)
            sv = _mm(ws_ref[g], v_ref[rows, cols]) + bias_ref[:, cols]
            t_ref[rows, cols] = (u_ref[rows, cols] * sv).astype(BF16)
        y = alpha * x_ref[rows, :] + _mm(t_ref[rows, :], wo_ref[...])
        o_ref[rows, :] = _layer_norm(y, g_ref[...], b_ref[...])


def _gmlp_out(u, v, x, w_s, bias, w_o, ln_g, ln_b, alpha):
    m, d = x.shape
    width = u.shape[1]
    groups = w_s.shape[0]
    tm = _row_tile(m, 512)
    kern = functools.partial(_gmlp_out_kernel, groups=groups, alpha=alpha)
    return pl.pallas_call(
        kern,
        grid=(m // tm,),
        in_specs=[
            pl.BlockSpec((tm, width), lambda i: (i, 0)),
            pl.BlockSpec((tm, width), lambda i: (i, 0)),
            pl.BlockSpec((tm, d), lambda i: (i, 0)),
            _resident(w_s.shape),
            _resident(bias.shape),
            _resident(w_o.shape),
            _resident(ln_g.shape),
            _resident(ln_b.shape),
        ],
        out_specs=pl.BlockSpec((tm, d), lambda i: (i, 0)),
        out_shape=jax.ShapeDtypeStruct((m, d), F32),
        scratch_shapes=[pltpu.VMEM((tm, width), BF16)],
        compiler_params=_params("parallel"),
        name="gmlp_out",
    )(u, v, x, w_s, bias, w_o, ln_g, ln_b)


def _ffn_kernel(x_ref, wg_ref, wu_ref, wd_ref, g_ref, b_ref, o_ref, xb_ref, act_ref, *, alpha, nc):
    f = pl.program_id(1)

    @pl.when(f == 0)
    def _():
        xb_ref[...] = x_ref[...].astype(BF16)
        o_ref[...] = alpha * x_ref[...]

    hc = min(wg_ref.shape[1], 2 * LANES)
    for h0 in range(0, wg_ref.shape[1], hc):
        gate = _mm(xb_ref[...], wg_ref[:, h0:h0 + hc])
        up = _mm(xb_ref[...], wu_ref[:, h0:h0 + hc])
        act_ref[:, h0:h0 + hc] = (gate * _sigmoid(gate) * up).astype(BF16)
    for n in range(0, o_ref.shape[1], nc):
        o_ref[:, n:n + nc] += _mm(act_ref[...], wd_ref[:, n:n + nc])

    @pl.when(f == pl.num_programs(1) - 1)
    def _():
        for r in range(0, o_ref.shape[0], ROW_CHUNK):
            rows = slice(r, r + ROW_CHUNK)
            o_ref[rows, :] = _layer_norm(o_ref[rows, :], g_ref[...], b_ref[...])


def _ffn(x, w_gu, w_down, layer, ln_g, ln_b, alpha):
    m, d = x.shape
    dff = w_down.shape[1]
    tm = _row_tile(m, 1024)
    tf = _col_chunk(dff, 512)
    nf = dff // tf
    kern = functools.partial(_ffn_kernel, alpha=alpha, nc=_col_chunk(d))
    return pl.pallas_call(
        kern,
        grid=(m // tm, nf),
        in_specs=[
            pl.BlockSpec((tm, d), lambda i, f: (i, 0)),
            pl.BlockSpec((None, d, tf), lambda i, f: (layer, 0, f)),
            pl.BlockSpec((None, d, tf), lambda i, f: (layer, 0, nf + f)),
            pl.BlockSpec((None, tf, d), lambda i, f: (layer, f, 0)),
            _resident(ln_g.shape),
            _resident(ln_b.shape),
        ],
        out_specs=pl.BlockSpec((tm, d), lambda i, f: (i, 0)),
        out_shape=jax.ShapeDtypeStruct((m, d), F32),
        scratch_shapes=[pltpu.VMEM((tm, d), BF16), pltpu.VMEM((tm, tf), BF16)],
        compiler_params=_params("parallel", "arbitrary", vmem_limit=V7X_FFN_VMEM_LIMIT_BYTES),
        name="ffn",
    )(x, w_gu, w_gu, w_down, ln_g, ln_b)


def _ple_kernel(x_ref, p_ref, wg_ref, wp_ref, o_ref, *, nc):
    d = o_ref.shape[1]
    xb = x_ref[...].astype(BF16)
    pb = p_ref[...].astype(BF16)
    for n in range(0, d, nc):
        gate = _sigmoid(_mm(xb, wg_ref[:, n:n + nc]))
        o_ref[:, n:n + nc] = x_ref[:, n:n + nc] + gate * _mm(pb, wp_ref[:, n:n + nc])


def _ple(x, p, layer, w_gate, w_proj):
    m, d = x.shape
    pd = p.shape[2]
    tm = _row_tile(m, 512)
    kern = functools.partial(_ple_kernel, nc=_col_chunk(d))
    return pl.pallas_call(
        kern,
        grid=(m // tm,),
        in_specs=[
            pl.BlockSpec((tm, d), lambda i: (i, 0)),
            pl.BlockSpec((None, tm, pd), lambda i: (layer, i, 0)),
            _resident_layer(w_gate.shape, layer),
            _resident_layer(w_proj.shape, layer),
        ],
        out_specs=pl.BlockSpec((tm, d), lambda i: (i, 0)),
        out_shape=jax.ShapeDtypeStruct((m, d), F32),
        compiler_params=_params("parallel"),
        name="ple",
    )(x, p, w_gate, w_proj)


def _rope_tables(seq):
    half = ROT_DIM // 2
    inv_freq = ROPE_THETA ** (-jnp.arange(0, ROT_DIM, 2, dtype=F32) / ROT_DIM)
    ang = jnp.arange(seq, dtype=F32)[:, None] * inv_freq[None, :]
    cos, sin = jnp.cos(ang), jnp.sin(ang)
    rest = HEAD_DIM - ROT_DIM
    c = jnp.concatenate([cos, cos, jnp.ones((seq, rest), F32)], axis=1)
    sa = jnp.concatenate([-sin, jnp.zeros((seq, HEAD_DIM - half), F32)], axis=1)
    sb = jnp.concatenate([jnp.zeros((seq, half), F32), sin, jnp.zeros((seq, rest), F32)], axis=1)
    return c, sa, sb


def _tile_deinterleave(tab, tile, dil):
    if dil == 1:
        return tab
    seq, w = tab.shape
    return tab.reshape(seq // tile, tile // dil, dil, w).swapaxes(1, 2).reshape(seq, w)


def _qkv_kernel(x_ref, w_ref, c_ref, sa_ref, sb_ref, o_ref, xb_ref, *stage, heads, dil, nc):
    tm = x_ref.shape[0]
    lt = tm // dil
    aw = heads * HEAD_DIM

    if dil == 1:
        xb_ref[...] = x_ref[...].astype(BF16)
    else:
        s1 = dil if dil <= MAX_ROW_STRIDE else MAX_ROW_STRIDE // 2
        s2 = dil // s1
        assert s1 * s2 == dil and s2 <= MAX_ROW_STRIDE
        xs_ref, xt_ref = stage
        for c in range(x_ref.shape[1] // LANES):
            cols = slice(c * LANES, (c + 1) * LANES)
            xs_ref[...] = x_ref[:, cols]
            if s2 == 1:
                for r in range(dil):
                    xb_ref[r * lt:(r + 1) * lt, cols] = (
                        xs_ref[pl.ds(r, lt, stride=dil), :].astype(BF16))
                continue
            l1 = tm // s1
            for r1 in range(s1):
                xt_ref[r1 * l1:(r1 + 1) * l1, :] = xs_ref[pl.ds(r1, l1, stride=s1), :]
            for r1 in range(s1):
                for r2 in range(s2):
                    r = r2 * s1 + r1
                    xb_ref[r * lt:(r + 1) * lt, cols] = (
                        xt_ref[pl.ds(r1 * l1 + r2, lt, stride=s2), :].astype(BF16))

    half = ROT_DIM // 2
    for kind in range(N_QKV):
        for n in range(0, aw, nc):
            h = _mm(xb_ref[...], w_ref[:, kind * aw + n:kind * aw + n + nc])
            for hd in range(nc // HEAD_DIM):
                t = h[:, hd * HEAD_DIM:(hd + 1) * HEAD_DIM]
                if kind < 2:
                    t = (t * c_ref[...] + pltpu.roll(t, HEAD_DIM - half, 1) * sa_ref[...]
                         + pltpu.roll(t, half, 1) * sb_ref[...])
                tb = t.astype(BF16)
                col = n + hd * HEAD_DIM
                for r in range(dil):
                    base = (kind * dil + r) * aw + col
                    o_ref[:, base:base + HEAD_DIM] = tb[r * lt:(r + 1) * lt]


def _qkv(x, w_qkv, tables, group, dil, seq, heads):
    m, d = x.shape
    aw = heads * HEAD_DIM
    tm = _row_tile(seq, 1024)
    spt = seq // tm
    nc = min(aw, 2 * HEAD_DIM)
    kern = functools.partial(_qkv_kernel, heads=heads, dil=dil, nc=nc)
    tab_spec = pl.BlockSpec((tm, HEAD_DIM), lambda i: (i % spt, 0))
    tabs = [_tile_deinterleave(t, tm, dil) for t in tables]
    return pl.pallas_call(
        kern,
        grid=(m // tm,),
        in_specs=[
            pl.BlockSpec((tm, d), lambda i: (i, 0)),
            pl.BlockSpec((d, N_QKV * aw), lambda i: (0, group), pipeline_mode=pl.Buffered(1)),
            tab_spec, tab_spec, tab_spec,
        ],
        out_specs=pl.BlockSpec((tm // dil, N_QKV * dil * aw), lambda i: (i, 0)),
        out_shape=jax.ShapeDtypeStruct((m // dil, N_QKV * dil * aw), BF16),
        scratch_shapes=[pltpu.VMEM((tm, d), BF16)]
        + ([pltpu.VMEM((tm, LANES), F32)] * 2 if dil > 1 else []),
        compiler_params=_params("parallel"),
        name=f"qkv_rope_g{group}",
    )(x, w_qkv, *tabs)


def _band_attn_kernel(q_ref, kp_ref, kc_ref, kn_ref, vp_ref, vc_ref, vn_ref,
                      o_ref, lse_ref, kcat_ref, vcat_ref, *, heads, sub_len, half):
    tq = q_ref.shape[0]
    nk = tq + 2 * half
    blk = pl.program_id(2)
    kcat_ref[0:half] = kp_ref[...]
    kcat_ref[half:half + tq] = kc_ref[...]
    kcat_ref[half + tq:nk] = kn_ref[...]
    vcat_ref[0:half] = vp_ref[...]
    vcat_ref[half:half + tq] = vc_ref[...]
    vcat_ref[half + tq:nk] = vn_ref[...]

    sq = min(tq, 2 * half)
    wk = sq + 2 * half
    row = lax.broadcasted_iota(jnp.int32, (sq, wk), 0)
    col = lax.broadcasted_iota(jnp.int32, (sq, wk), 1)
    rel = col - row
    band = (rel >= 0) & (rel <= 2 * half)
    lane = lax.broadcasted_iota(jnp.int32, (sq, LANES), 1)
    units = q_ref.shape[1] // (heads * HEAD_DIM)
    for j in range(tq // sq):
        key = blk * tq + j * sq - half + col
        mask = band & (key >= 0) & (key < sub_len)
        rows = slice(j * sq, (j + 1) * sq)
        keys = slice(j * sq, j * sq + wk)
        for u in range(units):
            mx_all = jnp.zeros((sq, LANES), F32)
            den_all = jnp.ones((sq, LANES), F32)
            for hd in range(heads):
                c0 = (u * heads + hd) * HEAD_DIM
                cols = slice(c0, c0 + HEAD_DIM)
                s = lax.dot_general(q_ref[rows, cols], kcat_ref[keys, cols],
                                    (((1,), (1,)), ((), ())), preferred_element_type=F32)
                s = jnp.where(mask, s, NEG_BIG)
                mx = jnp.max(s, axis=-1, keepdims=True)
                p = jnp.exp2(s - mx)
                den = jnp.sum(p, axis=-1, keepdims=True)
                o = _mm(p.astype(BF16), vcat_ref[keys, cols])
                o_ref[rows, cols] = o / den
                mx_all = jnp.where(lane == hd, mx, mx_all)
                den_all = jnp.where(lane == hd, den, den_all)
            lse_ref[rows, u * LANES:(u + 1) * LANES] = mx_all * math.log(2.0) + jnp.log(den_all)


def _band_attn(qkv, group, dil, half, batch, seq, heads):
    aw = heads * HEAD_DIM
    sub_len = seq // dil
    per_step = ATTN_SUBBLOCKS_PER_STEP * 2 * half
    tq = _row_tile(sub_len, per_step)
    hb = tq // half
    n_halo = sub_len // half
    units = max(1, min(dil, per_step // tq))
    assert dil % units == 0
    nres = dil // units
    uw = units * aw
    view = qkv.reshape(batch, sub_len, N_QKV * dil * aw)

    def q_map(b, r, l):
        return (b, l, r)

    def cur_map(kind):
        return lambda b, r, l: (b, l, kind * nres + r)

    def prev_map(kind):
        return lambda b, r, l: (b, jnp.maximum(l * hb - 1, 0), kind * nres + r)

    def next_map(kind):
        return lambda b, r, l: (b, jnp.minimum((l + 1) * hb, n_halo - 1), kind * nres + r)

    cur = lambda kind: pl.BlockSpec((None, tq, uw), cur_map(kind))
    prev = lambda kind: pl.BlockSpec((None, half, uw), prev_map(kind))
    nxt = lambda kind: pl.BlockSpec((None, half, uw), next_map(kind))

    kern = functools.partial(_band_attn_kernel, heads=heads, sub_len=sub_len, half=half)
    o, lse = pl.pallas_call(
        kern,
        grid=(batch, nres, sub_len // tq),
        in_specs=[pl.BlockSpec((None, tq, uw), q_map),
                  prev(1), cur(1), nxt(1), prev(2), cur(2), nxt(2)],
        out_specs=[
            pl.BlockSpec((None, tq, uw), lambda b, r, l: (b, l, r)),
            pl.BlockSpec((None, tq, units * LANES), lambda b, r, l: (b, l, r)),
        ],
        out_shape=[
            jax.ShapeDtypeStruct((batch, sub_len, dil * aw), F32),
            jax.ShapeDtypeStruct((batch, sub_len, dil * LANES), F32),
        ],
        scratch_shapes=[pltpu.VMEM((tq + 2 * half, uw), BF16),
                        pltpu.VMEM((tq + 2 * half, uw), BF16)],
        compiler_params=_params("parallel", "parallel", "parallel"),
        name=f"band_attn_g{group}",
    )(view, view, view, view, view, view, view)
    return (o.reshape(batch * sub_len, dil * aw), lse.reshape(batch * sub_len, dil * LANES))


def _attn_out_kernel(*refs, dils, heads, alpha):
    n_groups = len(dils)
    o_refs = list(refs[:n_groups])
    l_refs = list(refs[n_groups:2 * n_groups])
    x_ref, wo_ref, g_ref, b_ref, out_ref = refs[2 * n_groups:2 * n_groups + 5]
    stage = refs[2 * n_groups + 5:]
    tm, d = out_ref.shape
    aw = heads * HEAD_DIM
    head_o = [[None] * heads for _ in dils]
    s = 0
    for g, dil in enumerate(dils):
        if dil == 1:
            for hd in range(heads):
                head_o[g][hd] = o_refs[g].at[:, hd * HEAD_DIM:(hd + 1) * HEAD_DIM]
            continue
        on_ref, ln_ref = stage[s], stage[s + 1]
        s += 2
        lt = tm // dil
        for r in range(dil):
            ln_ref[pl.ds(r, lt, stride=dil), :] = l_refs[g][:, r * LANES:(r + 1) * LANES]
            for hd in range(heads):
                col = r * aw + hd * HEAD_DIM
                on_ref[pl.ds(hd * tm + r, lt, stride=dil), :] = o_refs[g][:, col:col + HEAD_DIM]
        for hd in range(heads):
            head_o[g][hd] = on_ref.at[hd * tm:(hd + 1) * tm, :]
        l_refs[g] = ln_ref
    lane = lax.broadcasted_iota(jnp.int32, (ROW_CHUNK, LANES), 1)
    for r0 in range(0, tm, ROW_CHUNK):
        rows = slice(r0, r0 + ROW_CHUNK)
        ls = [r[rows, :] for r in l_refs]
        mx = functools.reduce(jnp.maximum, ls)
        es = [jnp.exp(l - mx) for l in ls]
        inv = 1.0 / functools.reduce(jnp.add, es)
        ws = [e * inv for e in es]
        parts = []
        for hd in range(heads):
            col = [jnp.sum(jnp.where(lane == hd, w, 0.0), axis=1, keepdims=True) for w in ws]
            acc = col[0] * head_o[0][hd][rows, :]
            for g in range(1, n_groups):
                acc = acc + col[g] * head_o[g][hd][rows, :]
            parts.append(acc.astype(BF16))
        a = jnp.concatenate(parts, axis=1)
        y = alpha * x_ref[rows, :] + _mm(a, wo_ref[...])
        out_ref[rows, :] = _layer_norm(y, g_ref[...], b_ref[...])


def _attn_out(os_, lses, dils, x, w_o, ln_g, ln_b, alpha, heads):
    m, d = x.shape
    aw = heads * HEAD_DIM
    tm = _row_tile(m, 512)
    kern = functools.partial(_attn_out_kernel, dils=tuple(dils), heads=heads, alpha=alpha)
    row = lambda w: pl.BlockSpec((tm, w), lambda i: (i, 0))
    packed = lambda w: [pl.BlockSpec((tm // dil, dil * w), lambda i: (i, 0)) for dil in dils]
    stage = []
    for dil in dils:
        if dil > 1:
            stage += [pltpu.VMEM((heads * tm, LANES), F32), pltpu.VMEM((tm, LANES), F32)]
    return pl.pallas_call(
        kern,
        grid=(m // tm,),
        in_specs=packed(aw) + packed(LANES) + [
            row(d), _resident(w_o.shape), _resident(ln_g.shape), _resident(ln_b.shape)],
        out_specs=row(d),
        out_shape=jax.ShapeDtypeStruct((m, d), F32),
        scratch_shapes=stage,
        compiler_params=_params("parallel"),
        name="attn_out",
    )(*os_, *lses, x, w_o, ln_g, ln_b)


def _trunk(x, p, w):
    batch, seq, d = x.shape
    depth = p.shape[0]
    alpha = (2.0 * depth) ** 0.25
    m = batch * seq
    heads = w["w_o_b"].shape[1] // HEAD_DIM
    tables = _rope_tables(seq)
    h = x.reshape(m, d)
    for i in range(depth):
        j = i // 2
        ln_g = w["ln_mix_g"][i][None]
        ln_b = w["ln_mix_b"][i][None]
        if i % 2 == 0:
            u, v = _gmlp_in(h, w["w_in_a"][j], w["ln_v_g"][j][None], w["ln_v_b"][j][None])
            h = _gmlp_out(u, v, h, w["w_s_a"][j], w["sgu_bias"][j], w["w_o_a"][j], ln_g, ln_b, alpha)
        else:
            os_, lses = [], []
            for g, (window, dil) in enumerate(DILATION_PAIRS):
                qkv = _qkv(h, w["w_qkv_b"][j], tables, g, dil, seq, heads)
                o, lse = _band_attn(qkv, g, dil, window // (2 * dil), batch, seq, heads)
                os_.append(o)
                lses.append(lse)
            dils = [dil for _, dil in DILATION_PAIRS]
            h = _attn_out(os_, lses, dils, h, w["w_o_b"][j], ln_g, ln_b, alpha, heads)
        h = _ffn(h, w["w_ffn_gu"], w["w_ffn_down"], i,
                 w["ln_ffn_g"][i][None], w["ln_ffn_b"][i][None], alpha)
        h = _ple(h, p.reshape(depth, m, -1), i, w["w_ple_gate"], w["w_ple_proj"])
    return h.reshape(batch, seq, d)


def kernel(x_prompt, x_sample, p_prompt, p_sample, w_in_a, ln_v_g, ln_v_b, w_s_a, b_s_a, w_o_a, w_qkv_b, w_o_b, ln_mix_g, ln_mix_b, w_ffn_gu, w_ffn_down, ln_ffn_g, ln_ffn_b, w_ple_gate, w_ple_proj):
    group_dim = w_in_a.shape[2] // 2 // w_s_a.shape[1]
    assert group_dim == GROUP_DIM and w_s_a.shape[2] == CHUNK
    sgu_bias = jnp.repeat(jnp.swapaxes(b_s_a, 1, 2), group_dim, axis=2)
    aw = w_o_b.shape[1]
    is_q = (jnp.arange(w_qkv_b.shape[2]) // aw) % N_QKV == 0
    q_scale = jnp.where(is_q, HEAD_DIM ** -0.5 * math.log2(math.e), 1.0).astype(F32)
    w_qkv_b = w_qkv_b * q_scale
    w = dict(
        w_in_a=w_in_a.astype(BF16), ln_v_g=ln_v_g, ln_v_b=ln_v_b,
        w_s_a=w_s_a.astype(BF16), sgu_bias=sgu_bias, w_o_a=w_o_a.astype(BF16),
        w_qkv_b=w_qkv_b.astype(BF16), w_o_b=w_o_b.astype(BF16),
        ln_mix_g=ln_mix_g, ln_mix_b=ln_mix_b,
        w_ffn_gu=w_ffn_gu.astype(BF16), w_ffn_down=w_ffn_down.astype(BF16),
        ln_ffn_g=ln_ffn_g, ln_ffn_b=ln_ffn_b,
        w_ple_gate=w_ple_gate.astype(BF16), w_ple_proj=w_ple_proj.astype(BF16),
    )
    return (_trunk(x_prompt, p_prompt, w), _trunk(x_sample, p_sample, w))
```
